```python
import math, functools
import jax, jax.numpy as jnp
from jax import lax
import numpy as np

D_MODEL = 1024
BATCH = 8
SEQ = 2048
DEPTH = 4
DEC_BATCH = 32
DEC_SEQ = 4
PAST_LEN = 8192
PAGE_SIZE = 128

HEAD_DIM = 64
H_A = 8
H_B = 8
W_A = H_A * HEAD_DIM
W_B = H_B * HEAD_DIM
W_C = D_MODEL // 2
CONV_W = 3
N_BRANCH = 3
D_FF = -(-8 * D_MODEL // (3 * 256)) * 256
N_IN = 3 * W_A + H_A + 3 * W_B + 3 * W_C
Q_BLOCK = 128
RMS_EPS = 1e-6
FORGET_BIAS_INIT = 3.0

kernel_name = 'hybrid_fox_stickbreak_shortconv_step'


def _in_split_points():
    sizes = [W_A, W_A, W_A, H_A, W_B, W_B, W_B, W_C, W_C, W_C]
    return [int(v) for v in np.cumsum(sizes)[:-1]]


def rmsnorm(x, g):
    xf = x.astype(jnp.float32)
    y = xf * lax.rsqrt(jnp.mean(xf * xf, axis=-1, keepdims=True) + RMS_EPS)
    return (y * g.astype(jnp.float32)).astype(x.dtype)


def fox_block(q, cq, qpos, k, v, ck, kpos):
    s = jnp.einsum('bqhd,bkhd->bhqk', q, k).astype(jnp.float32) * (HEAD_DIM ** -0.5)
    decay = (jnp.transpose(cq, (0, 2, 1))[:, :, :, None]
             - jnp.transpose(ck, (0, 2, 1))[:, :, None, :])
    causal = kpos[None, :] <= qpos[:, None]
    p = jax.nn.softmax(jnp.where(causal, s + decay, -jnp.inf), axis=-1)
    return jnp.einsum('bhqk,bkhd->bqhd', p.astype(v.dtype), v)


def sb_block(q, qpos, k, v, kpos):
    z = jnp.einsum('bqhd,bkhd->bhqk', q, k).astype(jnp.float32) * (HEAD_DIM ** -0.5)
    strict = kpos[None, :] < qpos[:, None]
    log_rest = jnp.where(strict, jax.nn.log_sigmoid(-z), 0.0)
    log_between = lax.cumsum(log_rest, axis=3, reverse=True) - log_rest
    a = jnp.where(strict, jnp.exp(jax.nn.log_sigmoid(z) + log_between), 0.0)
    return jnp.einsum('bhqk,bkhd->bqhd', a.astype(v.dtype), v)


def sweep_query_blocks(block_fn, q_arrays, t):
    nb = t // Q_BLOCK
    def split(a):
        return jnp.moveaxis(a.reshape(a.shape[0], nb, Q_BLOCK, *a.shape[2:]), 1, 0)
    qpos = jnp.arange(t).reshape(nb, Q_BLOCK)
    out = lax.map(lambda args: block_fn(*args), (*[split(a) for a in q_arrays], qpos))
    out = jnp.moveaxis(out, 0, 1)
    return out.reshape(out.shape[0], t, *out.shape[3:])


def prompt_attend(qa, ka, va, logf, qb, kb, vb):
    t = qa.shape[1]
    pos = jnp.arange(t)
    c = jnp.cumsum(logf, axis=1)
    oa = sweep_query_blocks(lambda q, cq, qp: fox_block(q, cq, qp, ka, va, c, pos), (qa, c), t)
    ob = sweep_query_blocks(lambda q, qp: sb_block(q, qp, kb, vb, pos), (qb,), t)
    return oa, ob


def sample_attend(past, qa, ka, va, logf, qb, kb, vb):
    pka, pva, plogf, pkb, pvb = past
    p_len = pka.shape[1]
    t = qa.shape[1]
    kpos = jnp.arange(p_len + t)
    qpos = p_len + jnp.arange(t)
    ka_all = jnp.concatenate([pka, ka.astype(pka.dtype)], axis=1)
    va_all = jnp.concatenate([pva, va.astype(pva.dtype)], axis=1)
    c = jnp.cumsum(jnp.concatenate([plogf.astype(jnp.float32), logf], axis=1), axis=1)
    oa = fox_block(qa, c[:, p_len:], qpos, ka_all, va_all, c, kpos)
    kb_all = jnp.concatenate([pkb, kb.astype(pkb.dtype)], axis=1)
    vb_all = jnp.concatenate([pvb, vb.astype(pvb.dtype)], axis=1)
    ob = sb_block(qb, qpos, kb_all, vb_all, kpos)
    return oa, ob


def gather_pages(pool, page_table):
    g = pool[page_table]
    return g.reshape(g.shape[0], g.shape[1] * g.shape[2], *g.shape[3:])


def trunk_layer(x, attend, conv_buf, w_in, b_f, conv_w, w_proj_a, w_proj_b, w_proj_c,
                w_gate, b_gate, w_o, g_mix_pre, g_mix_post, g_ffn_pre, g_ffn_post,
                w_ffn_gate, w_ffn_up, w_ffn_down):
    bsz, t, _ = x.shape
    h = rmsnorm(x, g_mix_pre)
    qa, ka, va, f_logit, qb, kb, vb, gate_b, gate_c, hc = jnp.split(h @ w_in, _in_split_points(), axis=-1)
    heads = lambda a, n: a.reshape(bsz, t, n, HEAD_DIM)
    qa, ka, va = heads(qa, H_A), heads(ka, H_A), heads(va, H_A)
    qb, kb, vb = heads(qb, H_B), heads(kb, H_B), heads(vb, H_B)
    logf = jax.nn.log_sigmoid(f_logit.astype(jnp.float32) + b_f.astype(jnp.float32))
    oa, ob = attend(qa, ka, va, logf, qb, kb, vb)
    u = gate_c * hc
    u_pad = jnp.concatenate([conv_buf.astype(u.dtype), u], axis=1)
    conv = sum(conv_w[i] * u_pad[:, i:i + t] for i in range(CONV_W))
    yc = gate_b * conv
    new_buf = u_pad[:, u_pad.shape[1] - (CONV_W - 1):]
    ga, gb, gc = jnp.split(jax.nn.sigmoid(h @ w_gate + b_gate), N_BRANCH, axis=-1)
    mixed = (ga * (oa.reshape(bsz, t, W_A) @ w_proj_a)
             + gb * (ob.reshape(bsz, t, W_B) @ w_proj_b)
             + gc * (yc @ w_proj_c))
    x = x + rmsnorm(mixed @ w_o, g_mix_post)
    h2 = rmsnorm(x, g_ffn_pre)
    ffn = (jax.nn.silu(h2 @ w_ffn_gate) * (h2 @ w_ffn_up)) @ w_ffn_down
    x = x + rmsnorm(ffn, g_ffn_post)
    return x, (ka, va, logf, kb, vb, new_buf)


def setup_inputs(seed: int = 0) -> dict:
    key = jax.random.key(seed)
    ks = jax.random.split(key, 32)
    n_pages = PAST_LEN // PAGE_SIZE
    n_used = DEC_BATCH * n_pages
    n_pool = n_used + n_used // 4
    nrm = lambda k, shape, scale=1.0: scale * jax.random.normal(k, shape, jnp.float32)
    page_table = jax.random.permutation(ks[0], n_pool)[:n_used].reshape(DEC_BATCH, n_pages).astype(jnp.int32)
    return {
        'x_prompt': nrm(ks[1], (BATCH, SEQ, D_MODEL)),
        'x_sample': nrm(ks[2], (DEC_BATCH, DEC_SEQ, D_MODEL)),
        'cache_a_k': nrm(ks[3], (DEPTH, n_pool, PAGE_SIZE, H_A, HEAD_DIM)),
        'cache_a_v': nrm(ks[4], (DEPTH, n_pool, PAGE_SIZE, H_A, HEAD_DIM)),
        'cache_a_logf': jax.nn.log_sigmoid(FORGET_BIAS_INIT + nrm(ks[5], (DEPTH, n_pool, PAGE_SIZE, H_A))),
        'cache_b_k': nrm(ks[6], (DEPTH, n_pool, PAGE_SIZE, H_B, HEAD_DIM)),
        'cache_b_v': nrm(ks[7], (DEPTH, n_pool, PAGE_SIZE, H_B, HEAD_DIM)),
        'state_conv': nrm(ks[8], (DEPTH, DEC_BATCH, CONV_W - 1, W_C)),
        'page_table': page_table,
        'w_in': nrm(ks[9], (DEPTH, D_MODEL, N_IN), D_MODEL ** -0.5),
        'b_f': FORGET_BIAS_INIT + nrm(ks[10], (DEPTH, H_A)),
        'conv_w': nrm(ks[11], (DEPTH, CONV_W, W_C), CONV_W ** -0.5),
        'w_proj_a': nrm(ks[12], (DEPTH, W_A, D_MODEL), W_A ** -0.5),
        'w_proj_b': nrm(ks[13], (DEPTH, W_B, D_MODEL), W_B ** -0.5),
        'w_proj_c': nrm(ks[14], (DEPTH, W_C, D_MODEL), W_C ** -0.5),
        'w_gate': nrm(ks[15], (DEPTH, D_MODEL, N_BRANCH * D_MODEL), D_MODEL ** -0.5),
        'b_gate': nrm(ks[16], (DEPTH, N_BRANCH * D_MODEL), 0.01),
        'w_o': nrm(ks[17], (DEPTH, D_MODEL, D_MODEL), D_MODEL ** -0.5),
        'g_mix_pre': 1.0 + nrm(ks[18], (DEPTH, D_MODEL), 0.01),
        'g_mix_post': 1.0 + nrm(ks[19], (DEPTH, D_MODEL), 0.01),
        'g_ffn_pre': 1.0 + nrm(ks[20], (DEPTH, D_MODEL), 0.01),
        'g_ffn_post': 1.0 + nrm(ks[21], (DEPTH, D_MODEL), 0.01),
        'w_ffn_gate': nrm(ks[22], (DEPTH, D_MODEL, D_FF), D_MODEL ** -0.5),
        'w_ffn_up': nrm(ks[23], (DEPTH, D_MODEL, D_FF), D_MODEL ** -0.5),
        'w_ffn_down': nrm(ks[24], (DEPTH, D_FF, D_MODEL), D_FF ** -0.5),
    }


def reference(x_prompt, x_sample, cache_a_k, cache_a_v, cache_a_logf, cache_b_k, cache_b_v,
              state_conv, page_table, w_in, b_f, conv_w, w_proj_a, w_proj_b, w_proj_c,
              w_gate, b_gate, w_o, g_mix_pre, g_mix_post, g_ffn_pre, g_ffn_post,
              w_ffn_gate, w_ffn_up, w_ffn_down):
    weights = (w_in, b_f, conv_w, w_proj_a, w_proj_b, w_proj_c, w_gate, b_gate, w_o,
               g_mix_pre, g_mix_post, g_ffn_pre, g_ffn_post, w_ffn_gate, w_ffn_up, w_ffn_down)
    xp, xs = x_prompt, x_sample
    conv0 = jnp.zeros((x_prompt.shape[0], CONV_W - 1, W_C), x_prompt.dtype)
    rows_p, rows_s = [], []
    for l in range(DEPTH):
        lw = tuple(w[l] for w in weights)
        xp, new_p = trunk_layer(xp, prompt_attend, conv0, *lw)
        past = (gather_pages(cache_a_k[l], page_table), gather_pages(cache_a_v[l], page_table),
                gather_pages(cache_a_logf[l], page_table),
                gather_pages(cache_b_k[l], page_table), gather_pages(cache_b_v[l], page_table))
        xs, new_s = trunk_layer(xs, functools.partial(sample_attend, past), state_conv[l], *lw)
        rows_p.append(new_p)
        rows_s.append(new_s)
    stk = lambda rows, i: jnp.stack([r[i] for r in rows], axis=0)
    return (xp, xs,
            stk(rows_p, 0), stk(rows_p, 1), stk(rows_p, 2), stk(rows_p, 3), stk(rows_p, 4), stk(rows_p, 5),
            stk(rows_s, 0), stk(rows_s, 1), stk(rows_s, 2), stk(rows_s, 3), stk(rows_s, 4), stk(rows_s, 5))
```

```python
import functools

import jax
import jax.numpy as jnp
from jax import lax
from jax.experimental import pallas as pl
from jax.experimental.pallas import tpu as pltpu

D_MODEL = 1024
HEAD_DIM = 64
N_HEADS = 8
WIDTH = N_HEADS * HEAD_DIM
CONV_W = 3
D_FF = 2816
N_BRANCH = 3
PAGE_SIZE = 128
RMS_EPS = 1e-6
SCALE = HEAD_DIM ** -0.5

LANES = 128
SUBLANES = 8
VMEM_LIMIT_BYTES = 52 * 1024 * 1024

BF16 = jnp.bfloat16
F32 = jnp.float32
NEG = -1e30

ATT_BLOCK = 256
PAGES_PER_STEP = 4


def _nn(a, b):
    return jnp.dot(a, b, preferred_element_type=F32)


def _nt(a, b):
    return lax.dot_general(a, b, (((1,), (1,)), ((), ())), preferred_element_type=F32)


def _rms(x, g):
    ms = jnp.mean(x * x, axis=-1, keepdims=True)
    return x * lax.rsqrt(ms + RMS_EPS) * g


def _log_sigmoid(x):
    return jnp.minimum(x, 0.0) - jnp.log1p(jnp.exp(-jnp.abs(x)))


def _split_bf16(x, n):
    parts = []
    r = x
    for i in range(n):
        p = r.astype(BF16)
        parts.append(p)
        if i + 1 < n:
            r = r - p.astype(F32)
    return parts


def _params(n_axes):
    return pltpu.CompilerParams(dimension_semantics=("arbitrary",) * n_axes,
                                vmem_limit_bytes=VMEM_LIMIT_BYTES)


def _in_proj_kernel(x_ref, g_ref, w_ref, wf_ref, wft_ref, bf_ref, bft_ref, *outs, kv_bf16):
    h = _rms(x_ref[...], g_ref[...]).astype(BF16)
    outs = iter(outs)

    def chunk(k):
        return _nn(h, w_ref[:, WIDTH * k:WIDTH * (k + 1)])

    for base in (0, 3):
        q_ref = next(outs)
        q_ref[...] = (chunk(base) * SCALE).astype(q_ref.dtype)
        for k in (base + 1, base + 2):
            y = chunk(k)
            next(outs)[...] = y
            if kv_bf16:
                next(outs)[...] = y.astype(BF16)
    next(outs)[...] = chunk(6).astype(BF16)
    next(outs)[...] = chunk(7) * chunk(8)
    f = _nn(h, wf_ref[...])[:, :N_HEADS]
    next(outs)[...] = _log_sigmoid(f + bf_ref[...])
    ft = _nt(wft_ref[...], h)[:N_HEADS, :]
    next(outs)[...] = _log_sigmoid(ft + bft_ref[...])


def _in_proj(x, g, wcat, wf, wft, b_f, *, tm, q_dtype, kv_bf16):
    m = x.shape[0]
    row = lambda i: (i, 0)
    const = lambda i: (0, 0)
    wide = lambda dt: (jax.ShapeDtypeStruct((m, WIDTH), dt), pl.BlockSpec((tm, WIDTH), row))
    outs = []
    for _ in range(2):
        outs.append(wide(q_dtype))
        for _ in range(2):
            outs.append(wide(F32))
            if kv_bf16:
                outs.append(wide(BF16))
    outs.append(wide(BF16))
    outs.append(wide(F32))
    outs.append((jax.ShapeDtypeStruct((m, N_HEADS), F32), pl.BlockSpec((tm, N_HEADS), row)))
    outs.append((jax.ShapeDtypeStruct((N_HEADS, m), F32), pl.BlockSpec((N_HEADS, tm), lambda i: (0, i))))
    return pl.pallas_call(
        functools.partial(_in_proj_kernel, kv_bf16=kv_bf16),
        grid=(m // tm,),
        in_specs=[pl.BlockSpec((tm, D_MODEL), row),
                  pl.BlockSpec((1, D_MODEL), const),
                  pl.BlockSpec(wcat.shape, const),
                  pl.BlockSpec(wf.shape, const),
                  pl.BlockSpec(wft.shape, const),
                  pl.BlockSpec((1, N_HEADS), const),
                  pl.BlockSpec((N_HEADS, 1), const)],
        out_specs=[o[1] for o in outs],
        out_shape=[o[0] for o in outs],
        compiler_params=_params(1),
        name="in_proj",
    )(x, g, wcat, wf, wft, b_f.reshape(1, N_HEADS), b_f.reshape(N_HEADS, 1))


def _cumsum_kernel(lt_ref, c_ref, ct_ref, *, t_len, tq):
    nb = t_len // LANES
    x = jnp.concatenate([lt_ref[:, LANES * k:LANES * (k + 1)] for k in range(nb)], axis=0)
    n = nb * N_HEADS
    r = lax.broadcasted_iota(jnp.int32, (n, n), 0)
    c = lax.broadcasted_iota(jnp.int32, (n, n), 1)
    incl = (r <= c).astype(BF16)
    y = sum(_nn(p, incl) for p in _split_bf16(x, 3))
    earlier = ((c // N_HEADS < r // N_HEADS) & (c % N_HEADS == r % N_HEADS)).astype(BF16)
    offs = sum(_nn(earlier, p) for p in _split_bf16(y, 3))[:, LANES - 1:LANES]
    z = y + offs
    zt = z.T
    per_q = tq // LANES
    for k in range(nb):
        ct_ref[k // per_q, :, LANES * (k % per_q):LANES * (k % per_q + 1)] = z[N_HEADS * k:N_HEADS * (k + 1), :]
        c_ref[LANES * k:LANES * (k + 1), :] = zt[:, N_HEADS * k:N_HEADS * (k + 1)]


def _cumsum(logf_t, *, n_seq, t_len, tq):
    nq = t_len // tq
    return pl.pallas_call(
        functools.partial(_cumsum_kernel, t_len=t_len, tq=tq),
        grid=(n_seq,),
        in_specs=[pl.BlockSpec((N_HEADS, t_len), lambda b: (0, b))],
        out_specs=[pl.BlockSpec((t_len, N_HEADS), lambda b: (b, 0)),
                   pl.BlockSpec((None, nq, N_HEADS, tq), lambda b: (b, 0, 0, 0))],
        out_shape=[jax.ShapeDtypeStruct((n_seq * t_len, N_HEADS), F32),
                   jax.ShapeDtypeStruct((n_seq, nq, N_HEADS, tq), F32)],
        compiler_params=_params(1),
        name="logf_cumsum",
    )(logf_t)


def _head_pair_q(q_ref, tq):
    q = q_ref[...]
    low = lax.broadcasted_iota(jnp.int32, (tq, LANES), 1) < HEAD_DIM
    zero = jnp.zeros_like(q)
    return low, (jnp.where(low, q, zero), jnp.where(low, zero, q))


def _fox_kernel(q_ref, k_ref, v_ref, c_ref, ct_ref, o_ref, *, tq):
    pair = pl.program_id(1)
    qi = pl.program_id(2)
    low, qs = _head_pair_q(q_ref, tq)
    c_all = c_ref[...]
    hcol = lax.broadcasted_iota(jnp.int32, (tq, N_HEADS), 1)
    cq = [jnp.sum(jnp.where(hcol == 2 * pair + e, c_all, 0.0), axis=1, keepdims=True) for e in (0, 1)]
    row = lax.broadcasted_iota(jnp.int32, (tq, tq), 0)
    col = lax.broadcasted_iota(jnp.int32, (tq, tq), 1)
    causal = col <= row

    def block(j, carry, masked):
        start = pl.multiple_of(j * tq, tq)
        k = k_ref[pl.ds(start, tq), :]
        v = v_ref[pl.ds(start, tq), :]
        new = []
        for e in (0, 1):
            m, l, acc = carry[e]
            ck = ct_ref[j, pl.ds(2 * pair + e, 1), :]
            s = _nt(qs[e], k) + (cq[e] - ck)
            if masked:
                s = jnp.where(causal, s, NEG)
            m_new = jnp.maximum(m, jnp.max(s, axis=1, keepdims=True))
            alpha = jnp.exp(m - m_new)
            p = jnp.exp(s - m_new)
            l = alpha * l + jnp.sum(p, axis=1, keepdims=True)
            acc = alpha * acc + _nn(p.astype(BF16), v)
            new.append((m_new, l, acc))
        return tuple(new)

    init = tuple((jnp.full((tq, 1), NEG, F32), jnp.zeros((tq, 1), F32), jnp.zeros((tq, LANES), F32))
                 for _ in (0, 1))
    carry = lax.fori_loop(0, qi, lambda j, c: block(j, c, False), init)
    (_, l0, a0), (_, l1, a1) = block(qi, carry, True)
    o_ref[...] = jnp.where(low, a0 / l0, a1 / l1).astype(o_ref.dtype)


def _sb_kernel(q_ref, k_ref, v_ref, tri_ref, o_ref, *, tq):
    qi = pl.program_id(2)
    low, qs = _head_pair_q(q_ref, tq)
    row = lax.broadcasted_iota(jnp.int32, (tq, tq), 0)
    col = lax.broadcasted_iota(jnp.int32, (tq, tq), 1)
    strict = col < row
    tri = tri_ref[...]

    def block(j, carry, masked):
        start = pl.multiple_of(j * tq, tq)
        k = k_ref[pl.ds(start, tq), :]
        v = v_ref[pl.ds(start, tq), :]
        new = []
        for e in (0, 1):
            rest, acc = carry[e]
            z = _nt(qs[e], k)
            soft = jnp.log1p(jnp.exp(-jnp.abs(z)))
            ls_neg = jnp.minimum(-z, 0.0) - soft
            ls_pos = jnp.minimum(z, 0.0) - soft
            lr = jnp.where(strict, ls_neg, 0.0) if masked else ls_neg
            between = sum(_nn(p, tri) for p in _split_bf16(lr, 2)) + rest
            a = jnp.exp(ls_pos + between)
            if masked:
                a = jnp.where(strict, a, 0.0)
            acc = acc + _nn(a.astype(BF16), v)
            rest = rest + jnp.sum(lr, axis=1, keepdims=True)
            new.append((rest, acc))
        return tuple(new)

    init = tuple((jnp.zeros((tq, 1), F32), jnp.zeros((tq, LANES), F32)) for _ in (0, 1))
    carry = block(qi, init, True)
    (_, a0), (_, a1) = lax.fori_loop(0, qi, lambda i, c: block(qi - 1 - i, c, False), carry)
    o_ref[...] = jnp.where(low, a0, a1).astype(o_ref.dtype)


def _prompt_attention(kernel_fn, name, q, k, v, extra, extra_specs, *, n_seq, t_len, tq):
    nq = t_len // tq
    n_pair = WIDTH // LANES
    qspec = pl.BlockSpec((tq, LANES), lambda b, p, i: (b * nq + i, p))
    kvspec = pl.BlockSpec((t_len, LANES), lambda b, p, i: (b, p))
    return pl.pallas_call(
        functools.partial(kernel_fn, tq=tq),
        grid=(n_seq, n_pair, nq),
        in_specs=[qspec, kvspec, kvspec] + extra_specs,
        out_specs=qspec,
        out_shape=jax.ShapeDtypeStruct(q.shape, BF16),
        compiler_params=_params(3),
        name=name,
    )(q, k, v, *extra)


def _sample_attn_kernel(pt_ref, qa_ref, qb_ref, kan_ref, van_ref, kbn_ref, vbn_ref, lfn_ref,
                        tri_ref, sel_ref, hm_ref, *rest, n_pg, dec_b, dec_t):
    del pt_ref
    pg_ak, pg_av, pg_bk, pg_bv, pg_lf = (rest[n_pg * i:n_pg * (i + 1)] for i in range(5))
    oa_ref, ob_ref = rest[5 * n_pg:5 * n_pg + 2]
    (ka_s, va_s, kb_s, vb_s, lf_s, qa_s, qb_s, acca_s, accb_s,
     m_s, l_s, dec_s, rest_s) = rest[5 * n_pg + 2:]
    b = pl.program_id(0)
    j = pl.program_id(1)
    nk = n_pg * PAGE_SIZE
    nr = dec_t * N_HEADS
    hm = hm_ref[...]

    def token_rows(ref):
        return [ref[pl.ds(t * dec_b + b, 1), :] for t in range(dec_t)]

    @pl.when(j == 0)
    def _():
        def block_diag_q(ref):
            return jnp.concatenate([jnp.broadcast_to(r, (N_HEADS, WIDTH)) * hm for r in token_rows(ref)],
                                   axis=0).astype(BF16)

        def key_block(ref, width):
            sub = lax.broadcasted_iota(jnp.int32, (SUBLANES, width), 0)
            top = jnp.zeros((SUBLANES, width), F32)
            for t, r in enumerate(token_rows(ref)):
                top = jnp.where(sub == t, jnp.broadcast_to(r, (SUBLANES, width)), top)
            return jnp.concatenate([top, jnp.zeros((nk - SUBLANES, width), F32)], axis=0)

        qa_s[...] = block_diag_q(qa_ref)
        qb_s[...] = block_diag_q(qb_ref)
        ka_s[...] = key_block(kan_ref, WIDTH).astype(BF16)
        va_s[...] = key_block(van_ref, WIDTH).astype(BF16)
        kb_s[...] = key_block(kbn_ref, WIDTH).astype(BF16)
        vb_s[...] = key_block(vbn_ref, WIDTH).astype(BF16)
        lf_s[...] = jnp.zeros((nk, LANES), F32)
        lf_s[:, 0:N_HEADS] = key_block(lfn_ref, N_HEADS)
        acca_s[...] = jnp.zeros((nr, WIDTH), F32)
        accb_s[...] = jnp.zeros((nr, WIDTH), F32)
        m_s[...] = jnp.full((nr, 1), NEG, F32)
        l_s[...] = jnp.zeros((nr, 1), F32)
        dec_s[...] = jnp.zeros((nr, 1), F32)
        rest_s[...] = jnp.zeros((nr, 1), F32)

    @pl.when(j > 0)
    def _():
        for i in range(n_pg):
            rows = slice(PAGE_SIZE * i, PAGE_SIZE * (i + 1))
            ka_s[rows, :] = pg_ak[i][...].astype(BF16)
            va_s[rows, :] = pg_av[i][...].astype(BF16)
            kb_s[rows, :] = pg_bk[i][...].astype(BF16)
            vb_s[rows, :] = pg_bv[i][...].astype(BF16)
            lf_s[rows, 0:N_HEADS] = pg_lf[i][...]

    first = j == 0
    key = lax.broadcasted_iota(jnp.int32, (nr, nk), 1)
    t_row = lax.broadcasted_iota(jnp.int32, (nr, nk), 0) // N_HEADS
    valid_a = key < jnp.where(first, t_row + 1, nk)
    valid_b = key < jnp.where(first, t_row, nk)

    sa = _nt(qa_s[...], ka_s[...])
    sb = _nt(qb_s[...], kb_s[...])
    sel = sel_ref[...]
    lf = sum(_nt(sel, p) for p in _split_bf16(lf_s[...], 3))
    soft = jnp.log1p(jnp.exp(-jnp.abs(sb)))
    ls_neg = jnp.minimum(-sb, 0.0) - soft
    ls_pos = jnp.minimum(sb, 0.0) - soft
    lr = jnp.where(valid_b, ls_neg, 0.0)
    stacked = jnp.concatenate(_split_bf16(lf, 2) + _split_bf16(lr, 2), axis=0)
    later = _nn(stacked, tri_ref[...])
    later_a = later[0:nr] + later[nr:2 * nr]
    later_b = later[2 * nr:3 * nr] + later[3 * nr:4 * nr]

    at_query = jnp.sum(jnp.where(key == t_row, later_a, 0.0), axis=1, keepdims=True)
    dec = jnp.where(first, -at_query, dec_s[...])
    s = jnp.where(valid_a, sa + (later_a + dec), NEG)
    m_old = m_s[...]
    m_new = jnp.maximum(m_old, jnp.max(s, axis=1, keepdims=True))
    alpha = jnp.exp(m_old - m_new)
    p = jnp.exp(s - m_new)
    l_s[...] = alpha * l_s[...] + jnp.sum(p, axis=1, keepdims=True)
    acca_s[...] = alpha * acca_s[...] + _nn(p.astype(BF16), va_s[...])
    m_s[...] = m_new
    dec_s[...] = dec + jnp.sum(lf, axis=1, keepdims=True)

    a = jnp.where(valid_b, jnp.exp(ls_pos + later_b + rest_s[...]), 0.0)
    accb_s[...] = accb_s[...] + _nn(a.astype(BF16), vb_s[...])
    rest_s[...] = rest_s[...] + jnp.sum(lr, axis=1, keepdims=True)

    @pl.when(j == pl.num_programs(1) - 1)
    def _():
        oa = acca_s[...] / l_s[...]
        ob = accb_s[...]
        for t in range(dec_t):
            rows = slice(N_HEADS * t, N_HEADS * (t + 1))
            oa_ref[pl.ds(t * dec_b + b, 1), :] = jnp.sum(oa[rows] * hm, axis=0, keepdims=True)
            ob_ref[pl.ds(t * dec_b + b, 1), :] = jnp.sum(ob[rows] * hm, axis=0, keepdims=True)


def _sample_attention(layer, page_table, qa, qb, kan, van, kbn, vbn, lfn,
                      cache_a_k, cache_a_v, cache_a_logf, cache_b_k, cache_b_v, *, dec_b, dec_t):
    n_pg = PAGES_PER_STEP
    n_pages = page_table.shape[0] // dec_b
    nk = n_pg * PAGE_SIZE
    nr = dec_t * N_HEADS
    m = dec_b * dec_t
    r = lax.broadcasted_iota(jnp.int32, (nk, nk), 0)
    c = lax.broadcasted_iota(jnp.int32, (nk, nk), 1)
    tri = (r > c).astype(BF16)
    sel = (lax.broadcasted_iota(jnp.int32, (nr, LANES), 1)
           == lax.broadcasted_iota(jnp.int32, (nr, LANES), 0) % N_HEADS).astype(BF16)
    hm = (lax.broadcasted_iota(jnp.int32, (N_HEADS, WIDTH), 1) // HEAD_DIM
          == lax.broadcasted_iota(jnp.int32, (N_HEADS, WIDTH), 0)).astype(F32)

    def page_spec(i, width):
        def index(b, j, pt):
            pg = n_pages - jnp.maximum(j, 1) * n_pg + i
            return (layer, pt[b * n_pages + pg], 0, 0)
        return pl.BlockSpec((None, None, PAGE_SIZE, width), index)

    full = lambda shape: pl.BlockSpec(shape, lambda b, j, pt: (0,) * len(shape))
    caches = (cache_a_k, cache_a_v, cache_b_k, cache_b_v, cache_a_logf)
    widths = (WIDTH, WIDTH, WIDTH, WIDTH, N_HEADS)
    page_specs = [page_spec(i, w) for w in widths for i in range(n_pg)]
    page_args = [cch for cch in caches for _ in range(n_pg)]
    grid_spec = pltpu.PrefetchScalarGridSpec(
        num_scalar_prefetch=1,
        grid=(dec_b, 1 + n_pages // n_pg),
        in_specs=[full((m, WIDTH))] * 6 + [full((m, N_HEADS)), full((nk, nk)), full((nr, LANES)),
                                           full((N_HEADS, WIDTH))] + page_specs,
        out_specs=[full((m, WIDTH)), full((m, WIDTH))],
        scratch_shapes=[pltpu.VMEM((nk, WIDTH), BF16)] * 4 + [pltpu.VMEM((nk, LANES), F32)]
        + [pltpu.VMEM((nr, WIDTH), BF16)] * 2 + [pltpu.VMEM((nr, WIDTH), F32)] * 2
        + [pltpu.VMEM((nr, 1), F32)] * 4,
    )
    return pl.pallas_call(
        functools.partial(_sample_attn_kernel, n_pg=n_pg, dec_b=dec_b, dec_t=dec_t),
        grid_spec=grid_spec,
        out_shape=[jax.ShapeDtypeStruct((m, WIDTH), F32)] * 2,
        compiler_params=_params(2),
        name="sample_attention",
    )(page_table, qa, qb, kan, van, kbn, vbn, lfn, tri, sel, hm, *page_args)


def _merge_kernel(x_ref, oa_ref, ob_ref, gb_ref, u_ref, uprev_ref, st_ref, cw_ref, gpre_ref, gpost_ref,
                  wg_ref, bg_ref, wpa_ref, wpb_ref, wpc_ref, wo_ref, xo_ref, nb_ref, ext_s,
                  *, tm, shift, halo, tiles_per_seq):
    i = pl.program_id(0)
    x = x_ref[...]
    h = _rms(x, gpre_ref[...]).astype(BF16)
    u = u_ref[...]
    seq_start = i % tiles_per_seq == 0
    ext_s[0:halo, :] = jnp.where(seq_start, st_ref[0], uprev_ref[...])
    ext_s[halo:halo + tm, :] = u
    conv = (cw_ref[0:1, :] * ext_s[halo - 2 * shift:halo - 2 * shift + tm, :]
            + cw_ref[1:2, :] * ext_s[halo - shift:halo - shift + tm, :]
            + cw_ref[2:3, :] * u)
    yc = gb_ref[...].astype(F32) * conv
    mixed = jnp.zeros((tm, D_MODEL), F32)
    branches = ((oa_ref[...].astype(BF16), wpa_ref), (ob_ref[...].astype(BF16), wpb_ref), (yc.astype(BF16), wpc_ref))
    for br, (o, wp_ref) in enumerate(branches):
        cols = slice(D_MODEL * br, D_MODEL * (br + 1))
        gate = jax.nn.sigmoid(_nn(h, wg_ref[:, cols]) + bg_ref[:, cols])
        mixed = mixed + gate * _nn(o, wp_ref[...])
    y = _nn(mixed.astype(BF16), wo_ref[...])
    xo_ref[...] = x + _rms(y, gpost_ref[...])

    @pl.when(i % tiles_per_seq == tiles_per_seq - 1)
    def _():
        nb_ref[0] = ext_s[halo + tm - 2 * shift:halo + tm, :]


def _merge(x, oa, ob, gb, u, state_halo, conv_w, g_pre, g_post, w_gate, b_gate, wpa, wpb, wpc, wo,
           *, tm, shift, tiles_per_seq):
    m = x.shape[0]
    halo = state_halo.shape[1]
    n_groups = m // (tm * tiles_per_seq)
    row = lambda i: (i, 0)
    const = lambda i: (0, 0)
    wide = pl.BlockSpec((tm, WIDTH), row)
    return pl.pallas_call(
        functools.partial(_merge_kernel, tm=tm, shift=shift, halo=halo, tiles_per_seq=tiles_per_seq),
        grid=(m // tm,),
        in_specs=[pl.BlockSpec((tm, D_MODEL), row), wide, wide, wide, wide,
                  pl.BlockSpec((halo, WIDTH), lambda i: (jnp.maximum(i * (tm // halo) - 1, 0), 0)),
                  pl.BlockSpec((1, halo, WIDTH), lambda i: (i // tiles_per_seq, 0, 0)),
                  pl.BlockSpec((CONV_W, WIDTH), const),
                  pl.BlockSpec((1, D_MODEL), const), pl.BlockSpec((1, D_MODEL), const),
                  pl.BlockSpec(w_gate.shape, const), pl.BlockSpec((1, N_BRANCH * D_MODEL), const),
                  pl.BlockSpec(wpa.shape, const), pl.BlockSpec(wpb.shape, const),
                  pl.BlockSpec(wpc.shape, const), pl.BlockSpec(wo.shape, const)],
        out_specs=[pl.BlockSpec((tm, D_MODEL), row),
                   pl.BlockSpec((1, 2 * shift, WIDTH), lambda i: (i // tiles_per_seq, 0, 0))],
        out_shape=[jax.ShapeDtypeStruct((m, D_MODEL), F32),
                   jax.ShapeDtypeStruct((n_groups, 2 * shift, WIDTH), F32)],
        scratch_shapes=[pltpu.VMEM((halo + tm, WIDTH), F32)],
        compiler_params=_params(1),
        name="merge",
    )(x, oa, ob, gb, u, u, state_halo, conv_w, g_pre, g_post, w_gate, b_gate, wpa, wpb, wpc, wo)


def _ffn_kernel(x_ref, gpre_ref, gpost_ref, wg_ref, wu_ref, wd_ref, o_ref, h_s, acc_s):
    j = pl.program_id(1)

    @pl.when(j == 0)
    def _():
        h_s[...] = _rms(x_ref[...], gpre_ref[...]).astype(BF16)
        acc_s[...] = jnp.zeros_like(acc_s)

    h = h_s[...]
    g = _nn(h, wg_ref[...])
    up = _nn(h, wu_ref[...])
    act = (g * jax.nn.sigmoid(g) * up).astype(BF16)
    acc_s[...] += _nn(act, wd_ref[...])

    @pl.when(j == pl.num_programs(1) - 1)
    def _():
        o_ref[...] = x_ref[...] + _rms(acc_s[...], gpost_ref[...])


def _ffn(x, g_pre, g_post, wg, wu, wd, *, tm, tf):
    m = x.shape[0]
    row = lambda i, j: (i, 0)
    const = lambda i, j: (0, 0)
    return pl.pallas_call(
        _ffn_kernel,
        grid=(m // tm, D_FF // tf),
        in_specs=[pl.BlockSpec((tm, D_MODEL), row),
                  pl.BlockSpec((1, D_MODEL), const), pl.BlockSpec((1, D_MODEL), const),
                  pl.BlockSpec((D_MODEL, tf), lambda i, j: (0, j)),
                  pl.BlockSpec((D_MODEL, tf), lambda i, j: (0, j)),
                  pl.BlockSpec((tf, D_MODEL), lambda i, j: (j, 0))],
        out_specs=pl.BlockSpec((tm, D_MODEL), row),
        out_shape=jax.ShapeDtypeStruct((m, D_MODEL), F32),
        scratch_shapes=[pltpu.VMEM((tm, D_MODEL), BF16), pltpu.VMEM((tm, D_MODEL), F32)],
        compiler_params=_params(2),
        name="ffn",
    )(x, g_pre, g_post, wg, wu, wd)


def kernel(x_prompt, x_sample, cache_a_k, cache_a_v, cache_a_logf, cache_b_k, cache_b_v, state_conv, page_table, w_in, b_f, conv_w, w_proj_a, w_proj_b, w_proj_c, w_gate, b_gate, w_o, g_mix_pre, g_mix_post, g_ffn_pre, g_ffn_post, w_ffn_gate, w_ffn_up, w_ffn_down):
    n_seq, t_len, _ = x_prompt.shape
    dec_b, dec_t, _ = x_sample.shape
    depth = w_in.shape[0]
    n_pool = cache_a_k.shape[1]
    mp = n_seq * t_len
    ms = dec_b * dec_t
    tq = ATT_BLOCK
    tm_p = 512

    xp = x_prompt.reshape(mp, D_MODEL)
    xs = jnp.transpose(x_sample, (1, 0, 2)).reshape(ms, D_MODEL)
    pt = page_table.reshape(-1)
    caches = (cache_a_k.reshape(depth, n_pool, PAGE_SIZE, WIDTH),
              cache_a_v.reshape(depth, n_pool, PAGE_SIZE, WIDTH),
              cache_a_logf,
              cache_b_k.reshape(depth, n_pool, PAGE_SIZE, WIDTH),
              cache_b_v.reshape(depth, n_pool, PAGE_SIZE, WIDTH))
    r = lax.broadcasted_iota(jnp.int32, (tq, tq), 0)
    c = lax.broadcasted_iota(jnp.int32, (tq, tq), 1)
    tri = (r > c).astype(BF16)
    prompt_halo = jnp.zeros((n_seq, SUBLANES, WIDTH), F32)

    rows_p, rows_s = [], []
    qkv = 3 * WIDTH
    for l in range(depth):
        wl = w_in[l]
        wcat = jnp.concatenate([wl[:, 0:qkv], wl[:, qkv + N_HEADS:2 * qkv + N_HEADS],
                                wl[:, 2 * qkv + N_HEADS:]], axis=1).astype(BF16)
        w_f = wl[:, qkv:qkv + N_HEADS]
        wf = jnp.pad(w_f, ((0, 0), (0, LANES - N_HEADS))).astype(BF16)
        wft = jnp.pad(w_f.T, ((0, 2 * SUBLANES - N_HEADS), (0, 0))).astype(BF16)
        g_pre, g_post = g_mix_pre[l].reshape(1, -1), g_mix_post[l].reshape(1, -1)
        f_pre, f_post = g_ffn_pre[l].reshape(1, -1), g_ffn_post[l].reshape(1, -1)
        merge_w = (conv_w[l], g_pre, g_post, w_gate[l].astype(BF16), b_gate[l].reshape(1, -1),
                   w_proj_a[l].astype(BF16), w_proj_b[l].astype(BF16), w_proj_c[l].astype(BF16),
                   w_o[l].astype(BF16))
        ffn_w = (f_pre, f_post, w_ffn_gate[l].astype(BF16), w_ffn_up[l].astype(BF16),
                 w_ffn_down[l].astype(BF16))

        (qa, ka, ka16, va, va16, qb, kb, kb16, vb, vb16, gb, u, logf, logf_t) = _in_proj(
            xp, g_pre, wcat, wf, wft, b_f[l], tm=tm_p, q_dtype=BF16, kv_bf16=True)
        cum, cum_t = _cumsum(logf_t, n_seq=n_seq, t_len=t_len, tq=tq)
        nq = t_len // tq
        oa = _prompt_attention(
            _fox_kernel, "fox_attention", qa, ka16, va16, (cum, cum_t),
            [pl.BlockSpec((tq, N_HEADS), lambda b, p, i: (b * nq + i, 0)),
             pl.BlockSpec((None, nq, N_HEADS, tq), lambda b, p, i: (b, 0, 0, 0))],
            n_seq=n_seq, t_len=t_len, tq=tq)
        ob = _prompt_attention(
            _sb_kernel, "sb_attention", qb, kb16, vb16, (tri,),
            [pl.BlockSpec((tq, tq), lambda b, p, i: (0, 0))],
            n_seq=n_seq, t_len=t_len, tq=tq)
        xp, nb_p = _merge(xp, oa, ob, gb, u, prompt_halo, *merge_w,
                          tm=tm_p, shift=1, tiles_per_seq=t_len // tm_p)
        xp = _ffn(xp, *ffn_w, tm=1024, tf=256)
        rows_p.append((ka, va, logf, kb, vb, nb_p))

        (qa, ka, va, qb, kb, vb, gb, u, logf, _) = _in_proj(
            xs, g_pre, wcat, wf, wft, b_f[l], tm=ms, q_dtype=F32, kv_bf16=False)
        oa, ob = _sample_attention(l, pt, qa, qb, ka, va, kb, vb, logf, *caches,
                                   dec_b=dec_b, dec_t=dec_t)
        state = jnp.transpose(state_conv[l], (1, 0, 2)).reshape(1, (CONV_W - 1) * dec_b, WIDTH)
        xs, nb_s = _merge(xs, oa, ob, gb, u, state, *merge_w, tm=ms, shift=dec_b, tiles_per_seq=1)
        xs = _ffn(xs, *ffn_w, tm=ms, tf=256)
        rows_s.append((ka, va, logf, kb, vb, nb_s))

    def stack_p(i, tail):
        return jnp.stack([r[i] for r in rows_p], axis=0).reshape(depth, n_seq, -1, *tail)

    def stack_s(i, tail):
        a = jnp.stack([r[i] for r in rows_s], axis=0).reshape(depth, -1, dec_b, *tail)
        return jnp.swapaxes(a, 1, 2)

    heads = (N_HEADS, HEAD_DIM)
    return (xp.reshape(n_seq, t_len, D_MODEL),
            jnp.transpose(xs.reshape(dec_t, dec_b, D_MODEL), (1, 0, 2)),
            stack_p(0, heads), stack_p(1, heads), stack_p(2, (N_HEADS,)),
            stack_p(3, heads), stack_p(4, heads), stack_p(5, (WIDTH,)),
            stack_s(0, heads), stack_s(1, heads), stack_s(2, (N_HEADS,)),
            stack_s(3, heads), stack_s(4, heads), stack_s(5, (WIDTH,)))
```

```python
import functools

import jax
import jax.numpy as jnp
from jax import lax
from jax.experimental import pallas as pl
from jax.experimental.pallas import tpu as pltpu

D_MODEL = 1024
HEAD_DIM = 64
N_HEADS = 8
WIDTH = N_HEADS * HEAD_DIM
CONV_W = 3
D_FF = 2816
N_BRANCH = 3
PAGE_SIZE = 128
RMS_EPS = 1e-6
SCALE = HEAD_DIM ** -0.5
LOG2E = 1.4426950408889634

LANES = 128
SUBLANES = 8
VMEM_LIMIT_BYTES = 52 * 1024 * 1024

BF16 = jnp.bfloat16
F32 = jnp.float32
NEG = -1e30

ATT_BLOCK = 256
PAGES_PER_STEP = 4


def _nn(a, b):
    return jnp.dot(a, b, preferred_element_type=F32)


def _nt(a, b):
    return lax.dot_general(a, b, (((1,), (1,)), ((), ())), preferred_element_type=F32)


def _rms(x, g):
    ms = jnp.mean(x * x, axis=-1, keepdims=True)
    return x * lax.rsqrt(ms + RMS_EPS) * g


def _log_sigmoid(x):
    return jnp.minimum(x, 0.0) - jnp.log1p(jnp.exp(-jnp.abs(x)))


def _log_sigmoid_pair(z):
    soft = jnp.log1p(jnp.exp(-jnp.abs(z)))
    return jnp.minimum(-z, 0.0) - soft, jnp.minimum(z, 0.0) - soft


def _split_bf16(x, n):
    parts = []
    r = x
    for i in range(n):
        p = r.astype(BF16)
        parts.append(p)
        if i + 1 < n:
            r = r - p.astype(F32)
    return parts


def _params(n_axes):
    return pltpu.CompilerParams(dimension_semantics=("arbitrary",) * n_axes,
                                vmem_limit_bytes=VMEM_LIMIT_BYTES)


def _in_proj_prompt_kernel(x_ref, g_ref, wt_ref, wc_ref, wft_ref, bft_ref,
                           qat_ref, qbt_ref, ka_ref, kb_ref, kat_ref, vat_ref, kbt_ref, vbt_ref,
                           vat16_ref, vbt16_ref, gb_ref, u_ref, lft_ref, *, tq):
    h = _rms(x_ref[...], g_ref[...]).astype(BF16)
    tm = h.shape[0]
    rows = lambda k: wt_ref[WIDTH * k:WIDTH * (k + 1), :]
    for q_ref, kt_ref, k_ref, vt_ref, vt16_ref, base in ((qat_ref, kat_ref, ka_ref, vat_ref, vat16_ref, 0),
                                                         (qbt_ref, kbt_ref, kb_ref, vbt_ref, vbt16_ref, 3)):
        q_ref[...] = (_nt(rows(base), h) * (SCALE * LOG2E)).astype(BF16)
        kt_ref[...] = _nt(rows(base + 1), h)
        k_ref[...] = _nt(h, rows(base + 1)).astype(BF16)
        vt = _nt(rows(base + 2), h)
        vt_ref[...] = vt
        for c in range(tm // tq):
            vt16_ref[c] = vt[:, tq * c:tq * (c + 1)].astype(BF16)
    gb_ref[...] = _nn(h, wc_ref[:, 0:WIDTH]).astype(BF16)
    u_ref[...] = _nn(h, wc_ref[:, WIDTH:2 * WIDTH]) * _nn(h, wc_ref[:, 2 * WIDTH:3 * WIDTH])
    ft = _nt(wft_ref[...], h)[:N_HEADS, :]
    lft_ref[...] = _log_sigmoid(ft + bft_ref[...])


def _in_proj_prompt(x, g, wt, wc, wft, b_f, *, n_seq, t_len, tm, tq):
    m = x.shape[0]
    tps = t_len // tm
    nq = t_len // tq
    row = lambda i: (i, 0)
    const = lambda i: (0, 0)
    col = lambda i: (0, i)
    seq_t = lambda i: (i // tps, 0, i % tps)
    tok = lambda dt: (jax.ShapeDtypeStruct((m, WIDTH), dt), pl.BlockSpec((tm, WIDTH), row))
    feat = (jax.ShapeDtypeStruct((WIDTH, m), BF16), pl.BlockSpec((WIDTH, tm), col))
    new_t = (jax.ShapeDtypeStruct((n_seq, WIDTH, t_len), F32), pl.BlockSpec((None, WIDTH, tm), seq_t))
    blocked = (jax.ShapeDtypeStruct((n_seq, nq, WIDTH, tq), BF16),
               pl.BlockSpec((None, tm // tq, WIDTH, tq), lambda i: (i // tps, i % tps, 0, 0)))
    outs = [feat, feat, tok(BF16), tok(BF16), new_t, new_t, new_t, new_t, blocked, blocked,
            tok(BF16), tok(F32),
            (jax.ShapeDtypeStruct((n_seq, N_HEADS, t_len), F32), pl.BlockSpec((None, N_HEADS, tm), seq_t))]
    return pl.pallas_call(
        functools.partial(_in_proj_prompt_kernel, tq=tq),
        grid=(m // tm,),
        in_specs=[pl.BlockSpec((tm, D_MODEL), row),
                  pl.BlockSpec((1, D_MODEL), const),
                  pl.BlockSpec(wt.shape, const),
                  pl.BlockSpec(wc.shape, const),
                  pl.BlockSpec(wft.shape, const),
                  pl.BlockSpec((N_HEADS, 1), const)],
        out_specs=[o[1] for o in outs],
        out_shape=[o[0] for o in outs],
        compiler_params=_params(1),
        name="in_proj_prompt",
    )(x, g, wt, wc, wft, b_f.reshape(N_HEADS, 1))


def _in_proj_sample_kernel(x_ref, g_ref, wt_ref, wc_ref, wf_ref, wft_ref, bf_ref, bft_ref,
                           qa_ref, ka_ref, va_ref, qb_ref, kb_ref, vb_ref, gb_ref, u_ref, lf_ref, lft_ref):
    h = _rms(x_ref[...], g_ref[...]).astype(BF16)
    for k, o_ref in enumerate((qa_ref, ka_ref, va_ref, qb_ref, kb_ref, vb_ref)):
        y = _nt(h, wt_ref[WIDTH * k:WIDTH * (k + 1), :])
        o_ref[...] = y * SCALE if k % 3 == 0 else y
    gb_ref[...] = _nn(h, wc_ref[:, 0:WIDTH]).astype(BF16)
    u_ref[...] = _nn(h, wc_ref[:, WIDTH:2 * WIDTH]) * _nn(h, wc_ref[:, 2 * WIDTH:3 * WIDTH])
    f = _nn(h, wf_ref[...])[:, :N_HEADS]
    lf_ref[...] = _log_sigmoid(f + bf_ref[...])
    ft = _nt(wft_ref[...], h)[:N_HEADS, :]
    lft_ref[...] = _log_sigmoid(ft + bft_ref[...])


def _in_proj_sample(x, g, wt, wc, wf, wft, b_f):
    m = x.shape[0]
    full = lambda shape: pl.BlockSpec(shape, lambda i: (0,) * len(shape))
    tok = lambda dt: jax.ShapeDtypeStruct((m, WIDTH), dt)
    out_shape = [tok(F32)] * 6 + [tok(BF16), tok(F32),
                                  jax.ShapeDtypeStruct((m, N_HEADS), F32), jax.ShapeDtypeStruct((N_HEADS, m), F32)]
    args = (x, g, wt, wc, wf, wft, b_f.reshape(1, N_HEADS), b_f.reshape(N_HEADS, 1))
    return pl.pallas_call(
        _in_proj_sample_kernel,
        grid=(1,),
        in_specs=[full(a.shape) for a in args],
        out_specs=[full(s.shape) for s in out_shape],
        out_shape=out_shape,
        compiler_params=_params(1),
        name="in_proj_sample",
    )(*args)


def _cumsum_kernel(lt_ref, c_ref, ct_ref, *, t_len, tq):
    nb = t_len // LANES
    x = jnp.concatenate([lt_ref[:, LANES * k:LANES * (k + 1)] for k in range(nb)], axis=0)
    n = nb * N_HEADS
    r = lax.broadcasted_iota(jnp.int32, (LANES, LANES), 0)
    c = lax.broadcasted_iota(jnp.int32, (LANES, LANES), 1)
    incl = (r <= c).astype(BF16)
    y = sum(_nn(p, incl) for p in _split_bf16(x, 3))
    r = lax.broadcasted_iota(jnp.int32, (n, n), 0)
    c = lax.broadcasted_iota(jnp.int32, (n, n), 1)
    earlier = ((c // N_HEADS < r // N_HEADS) & (c % N_HEADS == r % N_HEADS)).astype(BF16)
    offs = sum(_nn(earlier, p) for p in _split_bf16(y, 3))[:, LANES - 1:LANES]
    z = (y + offs) * LOG2E
    zt = z.T
    per_q = tq // LANES
    for k in range(nb):
        ct_ref[k // per_q, :, LANES * (k % per_q):LANES * (k % per_q + 1)] = z[N_HEADS * k:N_HEADS * (k + 1), :]
        c_ref[LANES * k:LANES * (k + 1), :] = zt[:, N_HEADS * k:N_HEADS * (k + 1)]


def _cumsum(logf_t, *, tq):
    n_seq, _, t_len = logf_t.shape
    nq = t_len // tq
    return pl.pallas_call(
        functools.partial(_cumsum_kernel, t_len=t_len, tq=tq),
        grid=(n_seq,),
        in_specs=[pl.BlockSpec((None, N_HEADS, t_len), lambda b: (b, 0, 0))],
        out_specs=[pl.BlockSpec((t_len, N_HEADS), lambda b: (b, 0)),
                   pl.BlockSpec((None, nq, N_HEADS, tq), lambda b: (b, 0, 0, 0))],
        out_shape=[jax.ShapeDtypeStruct((n_seq * t_len, N_HEADS), F32),
                   jax.ShapeDtypeStruct((n_seq, nq, N_HEADS, tq), F32)],
        compiler_params=_params(1),
        name="logf_cumsum",
    )(logf_t)


def _head_pair_qt(qt_ref):
    qt = qt_ref[...]
    low = lax.broadcasted_iota(jnp.int32, qt.shape, 0) < HEAD_DIM
    zero = jnp.zeros_like(qt)
    return jnp.where(low, qt, zero), jnp.where(low, zero, qt)


def _head_column(c, head):
    hcol = lax.broadcasted_iota(jnp.int32, c.shape, 1)
    return jnp.sum(jnp.where(hcol == head, c, 0.0), axis=1, keepdims=True)


def _head_row(ct, head):
    hrow = lax.broadcasted_iota(jnp.int32, ct.shape, 0)
    return jnp.sum(jnp.where(hrow == head, ct, 0.0), axis=0, keepdims=True)


def _fox_kernel(qt_ref, k_ref, vt_ref, c_ref, ct_ref, o_ref, ck_s, s_s, *, tq):
    pair = pl.program_id(1)
    qi = pl.program_id(2)

    @pl.when(qi == 0)
    def _():
        c_all = c_ref[...]
        for e in (0, 1):
            ck_s[e] = jnp.broadcast_to(_head_column(c_all, 2 * pair + e), ck_s.shape[1:])

    qts = _head_pair_qt(qt_ref)
    cq_all = ct_ref[...]
    cq = [_head_row(cq_all, 2 * pair + e) for e in (0, 1)]
    key = lax.broadcasted_iota(jnp.int32, (tq, tq), 0)
    qry = lax.broadcasted_iota(jnp.int32, (tq, tq), 1)
    reps = tq // LANES
    n_full = qi // 2

    def logits(i, m, masked):
        js = (2 * i, 2 * i + 1)
        starts = [pl.multiple_of(j * tq, tq) for j in js]
        qk = [[_nn(k_ref[pl.ds(st, tq), :], qts[e]) for e in (0, 1)] for st in starts]
        m = list(m)
        for b, (j, st) in enumerate(zip(js, starts)):
            for e in (0, 1):
                ck = ck_s[e, pl.ds(st, tq), :]
                s = qk[b][e] + (cq[e] - jnp.concatenate([ck] * reps, axis=1))
                if masked:
                    s = jnp.where(key + (j - qi) * tq <= qry, s, NEG)
                s_s[e, j] = s
                m[e] = jnp.maximum(m[e], jnp.max(s, axis=0, keepdims=True))
        return tuple(m)

    m = lax.fori_loop(0, n_full, lambda i, m: logits(i, m, False), (jnp.full((1, tq), NEG, F32),) * 2)
    m = logits(n_full, m, True)

    def weigh(i, carry):
        js = (2 * i, 2 * i + 1)
        p = [[jnp.exp2(s_s[e, j] - m[e]) for e in (0, 1)] for j in js]
        pv = [[_nn(vt_ref[j, HEAD_DIM * e:HEAD_DIM * (e + 1), :], p[b][e].astype(BF16)) for e in (0, 1)]
              for b, j in enumerate(js)]
        return tuple((carry[e][0] + (jnp.sum(p[0][e], axis=0, keepdims=True) + jnp.sum(p[1][e], axis=0, keepdims=True)),
                      carry[e][1] + (pv[0][e] + pv[1][e])) for e in (0, 1))

    init = ((jnp.zeros((1, tq), F32), jnp.zeros((HEAD_DIM, tq), F32)),) * 2
    (l0, a0), (l1, a1) = lax.fori_loop(0, n_full + 1, weigh, init)
    o_ref[...] = jnp.concatenate([a0 / l0, a1 / l1], axis=0).T.astype(o_ref.dtype)


def _sb_kernel(qt_ref, k_ref, vt_ref, tri_ref, o_ref, lr_s, pos_s, tot_s, *, tq):
    qi = pl.program_id(2)
    qts = _head_pair_qt(qt_ref)
    key = lax.broadcasted_iota(jnp.int32, (tq, tq), 0)
    qry = lax.broadcasted_iota(jnp.int32, (tq, tq), 1)
    n_full = qi // 2

    def gates(i, masked):
        js = (2 * i, 2 * i + 1)
        z = [[_nn(k_ref[pl.ds(pl.multiple_of(j * tq, tq), tq), :], qts[e]) for e in (0, 1)] for j in js]
        for b, j in enumerate(js):
            for e in (0, 1):
                ze = z[b][e]
                lr = jnp.minimum(-ze, 0.0) - jnp.log2(1.0 + jnp.exp2(-jnp.abs(ze)))
                ls_pos = lr + ze
                if masked:
                    strict = key + (j - qi) * tq < qry
                    lr = jnp.where(strict, lr, 0.0)
                    ls_pos = jnp.where(strict, ls_pos, NEG)
                lr_s[e, j] = lr.astype(BF16)
                pos_s[e, j] = ls_pos
                tot_s[e, j] = jnp.sum(lr, axis=0, keepdims=True)

    def unmasked(i, carry):
        gates(i, False)
        return carry

    lax.fori_loop(0, n_full, unmasked, 0)
    gates(n_full, True)

    tri = tri_ref[...]

    def weigh(i, carry):
        js = (2 * (n_full - i) + 1, 2 * (n_full - i))
        between = [[_nn(tri, lr_s[e, j]) for e in (0, 1)] for j in js]
        rest = [(carry[e][0], carry[e][0] + tot_s[e, js[0]]) for e in (0, 1)]
        a = [[jnp.exp2(pos_s[e, j] + (between[b][e] + rest[e][b])).astype(BF16) for e in (0, 1)]
             for b, j in enumerate(js)]
        pv = [[_nn(vt_ref[j, HEAD_DIM * e:HEAD_DIM * (e + 1), :], a[b][e]) for e in (0, 1)]
              for b, j in enumerate(js)]
        return tuple((rest[e][1] + tot_s[e, js[1]], carry[e][1] + (pv[0][e] + pv[1][e])) for e in (0, 1))

    init = ((jnp.zeros((1, tq), F32), jnp.zeros((HEAD_DIM, tq), F32)),) * 2
    (_, a0), (_, a1) = lax.fori_loop(0, n_full + 1, weigh, init)
    o_ref[...] = jnp.concatenate([a0, a1], axis=0).T.astype(o_ref.dtype)


def _prompt_attention(kernel_fn, name, qt, k, vt16, extra, extra_specs, scratch, *, tq):
    n_seq, nq = vt16.shape[:2]
    assert nq % 2 == 0, "key blocks are consumed two at a time"
    t_len = nq * tq
    n_pair = WIDTH // LANES
    return pl.pallas_call(
        functools.partial(kernel_fn, tq=tq),
        grid=(n_seq, n_pair, nq),
        in_specs=[pl.BlockSpec((LANES, tq), lambda b, p, i: (p, b * nq + i)),
                  pl.BlockSpec((t_len, LANES), lambda b, p, i: (b, p)),
                  pl.BlockSpec((None, nq, LANES, tq), lambda b, p, i: (b, 0, p, 0))] + extra_specs,
        out_specs=pl.BlockSpec((tq, LANES), lambda b, p, i: (b * nq + i, p)),
        out_shape=jax.ShapeDtypeStruct(k.shape, BF16),
        scratch_shapes=scratch,
        compiler_params=_params(3),
        name=name,
    )(qt, k, vt16, *extra)


def _sample_attn_kernel(pt_ref, qa_ref, qb_ref, kan_ref, van_ref, kbn_ref, vbn_ref, lfn_ref,
                        tri_ref, hm_ref, *rest, n_pg, dec_b, dec_t):
    del pt_ref
    pg_ak, pg_av, pg_bk, pg_bv, pg_lf = (rest[n_pg * i:n_pg * (i + 1)] for i in range(5))
    oa_ref, ob_ref = rest[5 * n_pg:5 * n_pg + 2]
    (ka_s, va_s, kb_s, vb_s, lf_s, qa_s, qb_s, acca_s, accb_s,
     m_s, l_s, dec_s, rest_s) = rest[5 * n_pg + 2:]
    b = pl.program_id(0)
    j = pl.program_id(1)
    nk = n_pg * PAGE_SIZE
    nr = dec_t * N_HEADS
    hm = hm_ref[...]

    def token_rows(ref):
        return [ref[pl.ds(t * dec_b + b, 1), :] for t in range(dec_t)]

    @pl.when(j == 0)
    def _():
        def block_diag_q(ref):
            return jnp.concatenate([jnp.broadcast_to(r, (N_HEADS, WIDTH)) * hm for r in token_rows(ref)],
                                   axis=0).astype(BF16)

        def key_block_t(ref, s_ref):
            sub = lax.broadcasted_iota(jnp.int32, (SUBLANES, WIDTH), 0)
            top = jnp.zeros((SUBLANES, WIDTH), F32)
            for t, r in enumerate(token_rows(ref)):
                top = jnp.where(sub == t, jnp.broadcast_to(r, (SUBLANES, WIDTH)), top)
            page = jnp.concatenate([top, jnp.zeros((PAGE_SIZE - SUBLANES, WIDTH), F32)], axis=0)
            s_ref[:, 0:PAGE_SIZE] = page.T.astype(BF16)
            if nk > PAGE_SIZE:
                s_ref[:, PAGE_SIZE:nk] = jnp.zeros((WIDTH, nk - PAGE_SIZE), BF16)

        qa_s[...] = block_diag_q(qa_ref)
        qb_s[...] = block_diag_q(qb_ref)
        key_block_t(kan_ref, ka_s)
        key_block_t(van_ref, va_s)
        key_block_t(kbn_ref, kb_s)
        key_block_t(vbn_ref, vb_s)
        lane = lax.broadcasted_iota(jnp.int32, (N_HEADS, dec_b * dec_t), 1)
        pos = lax.broadcasted_iota(jnp.int32, (N_HEADS, nk), 1)
        lfn = lfn_ref[...]
        lf0 = jnp.zeros((N_HEADS, nk), F32)
        for t in range(dec_t):
            col = jnp.sum(jnp.where(lane == t * dec_b + b, lfn, 0.0), axis=1, keepdims=True)
            lf0 = jnp.where(pos == t, col, lf0)
        lf_s[...] = lf0
        acca_s[...] = jnp.zeros((nr, WIDTH), F32)
        accb_s[...] = jnp.zeros((nr, WIDTH), F32)
        m_s[...] = jnp.full((nr, 1), NEG, F32)
        l_s[...] = jnp.zeros((nr, 1), F32)
        dec_s[...] = jnp.zeros((nr, 1), F32)
        rest_s[...] = jnp.zeros((nr, 1), F32)

    @pl.when(j > 0)
    def _():
        for i in range(n_pg):
            cols = slice(PAGE_SIZE * i, PAGE_SIZE * (i + 1))
            ka_s[:, cols] = pg_ak[i][...].astype(BF16)
            va_s[:, cols] = pg_av[i][...].astype(BF16)
            kb_s[:, cols] = pg_bk[i][...].astype(BF16)
            vb_s[:, cols] = pg_bv[i][...].astype(BF16)
            lf_s[:, cols] = pg_lf[i][...]

    first = j == 0
    key = lax.broadcasted_iota(jnp.int32, (nr, nk), 1)
    t_row = lax.broadcasted_iota(jnp.int32, (nr, nk), 0) // N_HEADS
    valid_a = key < jnp.where(first, t_row + 1, nk)
    valid_b = key < jnp.where(first, t_row, nk)

    sa = _nn(qa_s[...], ka_s[...])
    sb = _nn(qb_s[...], kb_s[...])
    lf = jnp.concatenate([lf_s[...]] * dec_t, axis=0)
    ls_neg, ls_pos = _log_sigmoid_pair(sb)
    lr = jnp.where(valid_b, ls_neg, 0.0)
    stacked = jnp.concatenate(_split_bf16(lf, 2) + _split_bf16(lr, 2), axis=0)
    later = _nn(stacked, tri_ref[...])
    later_a = later[0:nr] + later[nr:2 * nr]
    later_b = later[2 * nr:3 * nr] + later[3 * nr:4 * nr]

    at_query = jnp.sum(jnp.where(key == t_row, later_a, 0.0), axis=1, keepdims=True)
    dec = jnp.where(first, -at_query, dec_s[...])
    s = jnp.where(valid_a, sa + (later_a + dec), NEG)
    m_old = m_s[...]
    m_new = jnp.maximum(m_old, jnp.max(s, axis=1, keepdims=True))
    alpha = jnp.exp(m_old - m_new)
    p = jnp.exp(s - m_new)
    l_s[...] = alpha * l_s[...] + jnp.sum(p, axis=1, keepdims=True)
    acca_s[...] = alpha * acca_s[...] + _nt(p.astype(BF16), va_s[...])
    m_s[...] = m_new
    dec_s[...] = dec + jnp.sum(lf, axis=1, keepdims=True)

    a = jnp.where(valid_b, jnp.exp(ls_pos + later_b + rest_s[...]), 0.0)
    accb_s[...] = accb_s[...] + _nt(a.astype(BF16), vb_s[...])
    rest_s[...] = rest_s[...] + jnp.sum(lr, axis=1, keepdims=True)

    @pl.when(j == pl.num_programs(1) - 1)
    def _():
        oa = acca_s[...] / l_s[...]
        ob = accb_s[...]
        for t in range(dec_t):
            rows = slice(N_HEADS * t, N_HEADS * (t + 1))
            oa_ref[pl.ds(t * dec_b + b, 1), :] = jnp.sum(oa[rows] * hm, axis=0, keepdims=True)
            ob_ref[pl.ds(t * dec_b + b, 1), :] = jnp.sum(ob[rows] * hm, axis=0, keepdims=True)


def _sample_attention(layer, page_table, qa, qb, kan, van, kbn, vbn, lfn_t,
                      cache_a_kt, cache_a_vt, cache_a_lft, cache_b_kt, cache_b_vt, *, dec_b, dec_t):
    n_pg = PAGES_PER_STEP
    n_pages = page_table.shape[0] // dec_b
    nk = n_pg * PAGE_SIZE
    nr = dec_t * N_HEADS
    m = dec_b * dec_t
    r = lax.broadcasted_iota(jnp.int32, (nk, nk), 0)
    c = lax.broadcasted_iota(jnp.int32, (nk, nk), 1)
    tri = (r > c).astype(BF16)
    hm = (lax.broadcasted_iota(jnp.int32, (N_HEADS, WIDTH), 1) // HEAD_DIM
          == lax.broadcasted_iota(jnp.int32, (N_HEADS, WIDTH), 0)).astype(F32)

    def page_spec(i, rows):
        def index(b, j, pt):
            pg = n_pages - jnp.maximum(j, 1) * n_pg + i
            return (layer, pt[b * n_pages + pg], 0, 0)
        return pl.BlockSpec((None, None, rows, PAGE_SIZE), index)

    full = lambda shape: pl.BlockSpec(shape, lambda b, j, pt: (0,) * len(shape))
    caches = (cache_a_kt, cache_a_vt, cache_b_kt, cache_b_vt, cache_a_lft)
    heights = (WIDTH, WIDTH, WIDTH, WIDTH, N_HEADS)
    page_specs = [page_spec(i, h) for h in heights for i in range(n_pg)]
    page_args = [cch for cch in caches for _ in range(n_pg)]
    grid_spec = pltpu.PrefetchScalarGridSpec(
        num_scalar_prefetch=1,
        grid=(dec_b, 1 + n_pages // n_pg),
        in_specs=[full((m, WIDTH))] * 6 + [full((N_HEADS, m)), full((nk, nk)), full((N_HEADS, WIDTH))]
        + page_specs,
        out_specs=[full((m, WIDTH)), full((m, WIDTH))],
        scratch_shapes=[pltpu.VMEM((WIDTH, nk), BF16)] * 4 + [pltpu.VMEM((N_HEADS, nk), F32)]
        + [pltpu.VMEM((nr, WIDTH), BF16)] * 2 + [pltpu.VMEM((nr, WIDTH), F32)] * 2
        + [pltpu.VMEM((nr, 1), F32)] * 4,
    )
    return pl.pallas_call(
        functools.partial(_sample_attn_kernel, n_pg=n_pg, dec_b=dec_b, dec_t=dec_t),
        grid_spec=grid_spec,
        out_shape=[jax.ShapeDtypeStruct((m, WIDTH), F32)] * 2,
        compiler_params=_params(2),
        name="sample_attention",
    )(page_table, qa, qb, kan, van, kbn, vbn, lfn_t, tri, hm, *page_args)


def _merge_kernel(x_ref, oa_ref, ob_ref, gb_ref, u_ref, uprev_ref, st_ref, cw_ref, gpre_ref, gpost_ref,
                  wg_ref, bg_ref, wpa_ref, wpb_ref, wpc_ref, wo_ref, xo_ref, nb_ref, ext_s,
                  *, tm, shift, halo, tiles_per_seq):
    i = pl.program_id(0)
    x = x_ref[...]
    h = _rms(x, gpre_ref[...]).astype(BF16)
    u = u_ref[...]
    seq_start = i % tiles_per_seq == 0
    ext_s[0:halo, :] = jnp.where(seq_start, st_ref[0], uprev_ref[...])
    ext_s[halo:halo + tm, :] = u
    conv = (cw_ref[0:1, :] * ext_s[halo - 2 * shift:halo - 2 * shift + tm, :]
            + cw_ref[1:2, :] * ext_s[halo - shift:halo - shift + tm, :]
            + cw_ref[2:3, :] * u)
    yc = gb_ref[...].astype(F32) * conv
    mixed = jnp.zeros((tm, D_MODEL), F32)
    branches = ((oa_ref[...].astype(BF16), wpa_ref), (ob_ref[...].astype(BF16), wpb_ref), (yc.astype(BF16), wpc_ref))
    for br, (o, wp_ref) in enumerate(branches):
        cols = slice(D_MODEL * br, D_MODEL * (br + 1))
        gate = jax.nn.sigmoid(_nn(h, wg_ref[:, cols]) + bg_ref[:, cols])
        mixed = mixed + gate * _nn(o, wp_ref[...])
    y = _nn(mixed.astype(BF16), wo_ref[...])
    xo_ref[...] = x + _rms(y, gpost_ref[...])

    @pl.when(i % tiles_per_seq == tiles_per_seq - 1)
    def _():
        nb_ref[0] = ext_s[halo + tm - 2 * shift:halo + tm, :]


def _merge(x, oa, ob, gb, u, state_halo, conv_w, g_pre, g_post, w_gate, b_gate, wpa, wpb, wpc, wo,
           *, tm, shift, tiles_per_seq):
    m = x.shape[0]
    halo = state_halo.shape[1]
    n_groups = m // (tm * tiles_per_seq)
    row = lambda i: (i, 0)
    const = lambda i: (0, 0)
    wide = pl.BlockSpec((tm, WIDTH), row)
    return pl.pallas_call(
        functools.partial(_merge_kernel, tm=tm, shift=shift, halo=halo, tiles_per_seq=tiles_per_seq),
        grid=(m // tm,),
        in_specs=[pl.BlockSpec((tm, D_MODEL), row), wide, wide, wide, wide,
                  pl.BlockSpec((halo, WIDTH), lambda i: (jnp.maximum(i * (tm // halo) - 1, 0), 0)),
                  pl.BlockSpec((1, halo, WIDTH), lambda i: (i // tiles_per_seq, 0, 0)),
                  pl.BlockSpec((CONV_W, WIDTH), const),
                  pl.BlockSpec((1, D_MODEL), const), pl.BlockSpec((1, D_MODEL), const),
                  pl.BlockSpec(w_gate.shape, const), pl.BlockSpec((1, N_BRANCH * D_MODEL), const),
                  pl.BlockSpec(wpa.shape, const), pl.BlockSpec(wpb.shape, const),
                  pl.BlockSpec(wpc.shape, const), pl.BlockSpec(wo.shape, const)],
        out_specs=[pl.BlockSpec((tm, D_MODEL), row),
                   pl.BlockSpec((1, 2 * shift, WIDTH), lambda i: (i // tiles_per_seq, 0, 0))],
        out_shape=[jax.ShapeDtypeStruct((m, D_MODEL), F32),
                   jax.ShapeDtypeStruct((n_groups, 2 * shift, WIDTH), F32)],
        scratch_shapes=[pltpu.VMEM((halo + tm, WIDTH), F32)],
        compiler_params=_params(1),
        name="merge",
    )(x, oa, ob, gb, u, u, state_halo, conv_w, g_pre, g_post, w_gate, b_gate, wpa, wpb, wpc, wo)


def _ffn_kernel(x_ref, gpre_ref, gpost_ref, wg_ref, wu_ref, wd_ref, o_ref, h_s, acc_s):
    j = pl.program_id(1)

    @pl.when(j == 0)
    def _():
        h_s[...] = _rms(x_ref[...], gpre_ref[...]).astype(BF16)
        acc_s[...] = jnp.zeros_like(acc_s)

    h = h_s[...]
    g = _nn(h, wg_ref[...])
    up = _nn(h, wu_ref[...])
    act = (g * jax.nn.sigmoid(g) * up).astype(BF16)
    acc_s[...] += _nn(act, wd_ref[...])

    @pl.when(j == pl.num_programs(1) - 1)
    def _():
        o_ref[...] = x_ref[...] + _rms(acc_s[...], gpost_ref[...])


def _ffn(x, g_pre, g_post, wg, wu, wd, *, tm, tf):
    m = x.shape[0]
    row = lambda i, j: (i, 0)
    const = lambda i, j: (0, 0)
    return pl.pallas_call(
        _ffn_kernel,
        grid=(m // tm, D_FF // tf),
        in_specs=[pl.BlockSpec((tm, D_MODEL), row),
                  pl.BlockSpec((1, D_MODEL), const), pl.BlockSpec((1, D_MODEL), const),
                  pl.BlockSpec((D_MODEL, tf), lambda i, j: (0, j)),
                  pl.BlockSpec((D_MODEL, tf), lambda i, j: (0, j)),
                  pl.BlockSpec((tf, D_MODEL), lambda i, j: (j, 0))],
        out_specs=pl.BlockSpec((tm, D_MODEL), row),
        out_shape=jax.ShapeDtypeStruct((m, D_MODEL), F32),
        scratch_shapes=[pltpu.VMEM((tm, D_MODEL), BF16), pltpu.VMEM((tm, D_MODEL), F32)],
        compiler_params=_params(2),
        name="ffn",
    )(x, g_pre, g_post, wg, wu, wd)


def kernel(x_prompt, x_sample, cache_a_k, cache_a_v, cache_a_logf, cache_b_k, cache_b_v, state_conv, page_table, w_in, b_f, conv_w, w_proj_a, w_proj_b, w_proj_c, w_gate, b_gate, w_o, g_mix_pre, g_mix_post, g_ffn_pre, g_ffn_post, w_ffn_gate, w_ffn_up, w_ffn_down):
    n_seq, t_len, _ = x_prompt.shape
    dec_b, dec_t, _ = x_sample.shape
    depth = w_in.shape[0]
    n_pool = cache_a_k.shape[1]
    mp = n_seq * t_len
    ms = dec_b * dec_t
    tq = ATT_BLOCK
    tm_p = 512

    xp = x_prompt.reshape(mp, D_MODEL)
    xs = jnp.transpose(x_sample, (1, 0, 2)).reshape(ms, D_MODEL)
    pt = page_table.reshape(-1)
    feature_major = lambda c: jnp.transpose(c, (0, 1, 3, 4, 2)).reshape(depth, n_pool, WIDTH, PAGE_SIZE)
    caches = (feature_major(cache_a_k), feature_major(cache_a_v), jnp.transpose(cache_a_logf, (0, 1, 3, 2)),
              feature_major(cache_b_k), feature_major(cache_b_v))
    r = lax.broadcasted_iota(jnp.int32, (tq, tq), 0)
    c = lax.broadcasted_iota(jnp.int32, (tq, tq), 1)
    tri = (c > r).astype(BF16)
    prompt_halo = jnp.zeros((n_seq, SUBLANES, WIDTH), F32)

    rows_p, rows_s = [], []
    qkv = 3 * WIDTH
    for l in range(depth):
        wl = w_in[l]
        wt = jnp.concatenate([wl[:, 0:qkv], wl[:, qkv + N_HEADS:2 * qkv + N_HEADS]], axis=1).T.astype(BF16)
        wc = wl[:, 2 * qkv + N_HEADS:].astype(BF16)
        w_f = wl[:, qkv:qkv + N_HEADS]
        wf = jnp.pad(w_f, ((0, 0), (0, LANES - N_HEADS))).astype(BF16)
        wft = jnp.pad(w_f.T, ((0, 2 * SUBLANES - N_HEADS), (0, 0))).astype(BF16)
        g_pre, g_post = g_mix_pre[l].reshape(1, -1), g_mix_post[l].reshape(1, -1)
        f_pre, f_post = g_ffn_pre[l].reshape(1, -1), g_ffn_post[l].reshape(1, -1)
        merge_w = (conv_w[l], g_pre, g_post, w_gate[l].astype(BF16), b_gate[l].reshape(1, -1),
                   w_proj_a[l].astype(BF16), w_proj_b[l].astype(BF16), w_proj_c[l].astype(BF16),
                   w_o[l].astype(BF16))
        ffn_w = (f_pre, f_post, w_ffn_gate[l].astype(BF16), w_ffn_up[l].astype(BF16),
                 w_ffn_down[l].astype(BF16))

        (qat, qbt, ka16, kb16, kat, vat, kbt, vbt, vat16, vbt16, gb, u, logf_t) = _in_proj_prompt(
            xp, g_pre, wt, wc, wft, b_f[l], n_seq=n_seq, t_len=t_len, tm=tm_p, tq=tq)
        cum, cum_t = _cumsum(logf_t, tq=tq)
        nq = t_len // tq
        oa = _prompt_attention(
            _fox_kernel, "fox_attention", qat, ka16, vat16, (cum, cum_t),
            [pl.BlockSpec((t_len, N_HEADS), lambda b, p, i: (b, 0)),
             pl.BlockSpec((None, None, N_HEADS, tq), lambda b, p, i: (b, i, 0, 0))],
            [pltpu.VMEM((2, t_len, LANES), F32), pltpu.VMEM((2, nq, tq, tq), F32)], tq=tq)
        ob = _prompt_attention(
            _sb_kernel, "sb_attention", qbt, kb16, vbt16, (tri,),
            [pl.BlockSpec((tq, tq), lambda b, p, i: (0, 0))],
            [pltpu.VMEM((2, nq, tq, tq), BF16), pltpu.VMEM((2, nq, tq, tq), F32),
             pltpu.VMEM((2, nq, 1, tq), F32)], tq=tq)
        xp, nb_p = _merge(xp, oa, ob, gb, u, prompt_halo, *merge_w,
                          tm=tm_p, shift=1, tiles_per_seq=t_len // tm_p)
        xp = _ffn(xp, *ffn_w, tm=1024, tf=256)
        rows_p.append((kat, vat, logf_t, kbt, vbt, nb_p))

        (qa, ka, va, qb, kb, vb, gb, u, logf, logf_t) = _in_proj_sample(
            xs, g_pre, wt, wc, wf, wft, b_f[l])
        oa, ob = _sample_attention(l, pt, qa, qb, ka, va, kb, vb, logf_t, *caches,
                                   dec_b=dec_b, dec_t=dec_t)
        state = jnp.transpose(state_conv[l], (1, 0, 2)).reshape(1, (CONV_W - 1) * dec_b, WIDTH)
        xs, nb_s = _merge(xs, oa, ob, gb, u, state, *merge_w, tm=ms, shift=dec_b, tiles_per_seq=1)
        xs = _ffn(xs, *ffn_w, tm=ms, tf=256)
        rows_s.append((ka, va, logf, kb, vb, nb_s))

    stack = lambda rows, i: jnp.stack([r[i] for r in rows], axis=0)

    def new_kv_p(i):
        a = stack(rows_p, i).reshape(depth, n_seq, N_HEADS, HEAD_DIM, t_len)
        return jnp.transpose(a, (0, 1, 4, 2, 3))

    def stack_s(i, tail):
        a = stack(rows_s, i).reshape(depth, -1, dec_b, *tail)
        return jnp.swapaxes(a, 1, 2)

    heads = (N_HEADS, HEAD_DIM)
    return (xp.reshape(n_seq, t_len, D_MODEL),
            jnp.transpose(xs.reshape(dec_t, dec_b, D_MODEL), (1, 0, 2)),
            new_kv_p(0), new_kv_p(1), jnp.transpose(stack(rows_p, 2), (0, 1, 3, 2)),
            new_kv_p(3), new_kv_p(4), stack(rows_p, 5),
            stack_s(0, heads), stack_s(1, heads), stack_s(2, (N_HEADS,)),
            stack_s(3, heads), stack_s(4, heads), stack_s(5, (WIDTH,)))
```

```python
import functools

import jax
import jax.numpy as jnp
from jax import lax
from jax.experimental import pallas as pl
from jax.experimental.pallas import tpu as pltpu

D_MODEL = 1024
HEAD_DIM = 64
N_HEADS = 8
WIDTH = N_HEADS * HEAD_DIM
CONV_W = 3
D_FF = 2816
N_BRANCH = 3
PAGE_SIZE = 128
RMS_EPS = 1e-6
SCALE = HEAD_DIM ** -0.5
LOG2E = 1.4426950408889634

LANES = 128
SUBLANES = 8
VMEM_LIMIT_BYTES = 52 * 1024 * 1024

BF16 = jnp.bfloat16
F32 = jnp.float32
NEG = -1e30

ATT_BLOCK = 256
PAGES_PER_STEP = 8
PAGES_PER_BLOCK = 4


def _nn(a, b):
    return jnp.dot(a, b, preferred_element_type=F32)


def _nt(a, b):
    return lax.dot_general(a, b, (((1,), (1,)), ((), ())), preferred_element_type=F32)


def _rms(x, g):
    ms = jnp.mean(x * x, axis=-1, keepdims=True)
    return x * lax.rsqrt(ms + RMS_EPS) * g


def _log_sigmoid(x):
    return jnp.minimum(x, 0.0) - jnp.log1p(jnp.exp(-jnp.abs(x)))


def _log_sigmoid_pair(z):
    soft = jnp.log1p(jnp.exp(-jnp.abs(z)))
    return jnp.minimum(-z, 0.0) - soft, jnp.minimum(z, 0.0) - soft


def _split_bf16(x, n):
    parts = []
    r = x
    for i in range(n):
        p = r.astype(BF16)
        parts.append(p)
        if i + 1 < n:
            r = r - p.astype(F32)
    return parts


def _params(n_axes):
    return pltpu.CompilerParams(dimension_semantics=("arbitrary",) * n_axes,
                                vmem_limit_bytes=VMEM_LIMIT_BYTES)


def _in_proj_prompt_kernel(x_ref, g_ref, wt_ref, wc_ref, wft_ref, bft_ref,
                           qat_ref, qbt_ref, ka_ref, kb_ref, kat_ref, vat_ref, kbt_ref, vbt_ref,
                           vat16_ref, vbt16_ref, gb_ref, u_ref, lft_ref, *, tq):
    h = _rms(x_ref[...], g_ref[...]).astype(BF16)
    tm = h.shape[0]
    rows = lambda k: wt_ref[WIDTH * k:WIDTH * (k + 1), :]
    for q_ref, kt_ref, k_ref, vt_ref, vt16_ref, base in ((qat_ref, kat_ref, ka_ref, vat_ref, vat16_ref, 0),
                                                         (qbt_ref, kbt_ref, kb_ref, vbt_ref, vbt16_ref, 3)):
        q_ref[...] = (_nt(rows(base), h) * (SCALE * LOG2E)).astype(BF16)
        kt_ref[...] = _nt(rows(base + 1), h)
        k_ref[...] = _nt(h, rows(base + 1)).astype(BF16)
        vt = _nt(rows(base + 2), h)
        vt_ref[...] = vt
        for c in range(tm // tq):
            vt16_ref[c] = vt[:, tq * c:tq * (c + 1)].astype(BF16)
    gb_ref[...] = _nn(h, wc_ref[:, 0:WIDTH]).astype(BF16)
    u_ref[...] = _nn(h, wc_ref[:, WIDTH:2 * WIDTH]) * _nn(h, wc_ref[:, 2 * WIDTH:3 * WIDTH])
    ft = _nt(wft_ref[...], h)[:N_HEADS, :]
    lft_ref[...] = _log_sigmoid(ft + bft_ref[...])


def _in_proj_prompt(x, g, wt, wc, wft, b_f, *, n_seq, t_len, tm, tq):
    m = x.shape[0]
    tps = t_len // tm
    nq = t_len // tq
    row = lambda i: (i, 0)
    const = lambda i: (0, 0)
    col = lambda i: (0, i)
    seq_t = lambda i: (i // tps, 0, i % tps)
    tok = lambda dt: (jax.ShapeDtypeStruct((m, WIDTH), dt), pl.BlockSpec((tm, WIDTH), row))
    feat = (jax.ShapeDtypeStruct((WIDTH, m), BF16), pl.BlockSpec((WIDTH, tm), col))
    new_t = (jax.ShapeDtypeStruct((n_seq, WIDTH, t_len), F32), pl.BlockSpec((None, WIDTH, tm), seq_t))
    blocked = (jax.ShapeDtypeStruct((n_seq, nq, WIDTH, tq), BF16),
               pl.BlockSpec((None, tm // tq, WIDTH, tq), lambda i: (i // tps, i % tps, 0, 0)))
    outs = [feat, feat, tok(BF16), tok(BF16), new_t, new_t, new_t, new_t, blocked, blocked,
            tok(BF16), tok(F32),
            (jax.ShapeDtypeStruct((n_seq, N_HEADS, t_len), F32), pl.BlockSpec((None, N_HEADS, tm), seq_t))]
    return pl.pallas_call(
        functools.partial(_in_proj_prompt_kernel, tq=tq),
        grid=(m // tm,),
        in_specs=[pl.BlockSpec((tm, D_MODEL), row),
                  pl.BlockSpec((1, D_MODEL), const),
                  pl.BlockSpec(wt.shape, const),
                  pl.BlockSpec(wc.shape, const),
                  pl.BlockSpec(wft.shape, const),
                  pl.BlockSpec((N_HEADS, 1), const)],
        out_specs=[o[1] for o in outs],
        out_shape=[o[0] for o in outs],
        compiler_params=_params(1),
        name="in_proj_prompt",
    )(x, g, wt, wc, wft, b_f.reshape(N_HEADS, 1))


def _in_proj_sample_kernel(x_ref, g_ref, wt_ref, wc_ref, wf_ref, wft_ref, bf_ref, bft_ref,
                           qa_ref, ka_ref, va_ref, qb_ref, kb_ref, vb_ref, gb_ref, u_ref, lf_ref, lft_ref):
    h = _rms(x_ref[...], g_ref[...]).astype(BF16)
    for k, o_ref in enumerate((qa_ref, ka_ref, va_ref, qb_ref, kb_ref, vb_ref)):
        y = _nt(h, wt_ref[WIDTH * k:WIDTH * (k + 1), :])
        o_ref[...] = y * SCALE if k % 3 == 0 else y
    gb_ref[...] = _nn(h, wc_ref[:, 0:WIDTH]).astype(BF16)
    u_ref[...] = _nn(h, wc_ref[:, WIDTH:2 * WIDTH]) * _nn(h, wc_ref[:, 2 * WIDTH:3 * WIDTH])
    f = _nn(h, wf_ref[...])[:, :N_HEADS]
    lf_ref[...] = _log_sigmoid(f + bf_ref[...])
    ft = _nt(wft_ref[...], h)[:N_HEADS, :]
    lft_ref[...] = _log_sigmoid(ft + bft_ref[...])


def _in_proj_sample(x, g, wt, wc, wf, wft, b_f):
    m = x.shape[0]
    full = lambda shape: pl.BlockSpec(shape, lambda i: (0,) * len(shape))
    tok = lambda dt: jax.ShapeDtypeStruct((m, WIDTH), dt)
    out_shape = [tok(F32)] * 6 + [tok(BF16), tok(F32),
                                  jax.ShapeDtypeStruct((m, N_HEADS), F32), jax.ShapeDtypeStruct((N_HEADS, m), F32)]
    args = (x, g, wt, wc, wf, wft, b_f.reshape(1, N_HEADS), b_f.reshape(N_HEADS, 1))
    return pl.pallas_call(
        _in_proj_sample_kernel,
        grid=(1,),
        in_specs=[full(a.shape) for a in args],
        out_specs=[full(s.shape) for s in out_shape],
        out_shape=out_shape,
        compiler_params=_params(1),
        name="in_proj_sample",
    )(*args)


def _cumsum_kernel(lt_ref, c_ref, ct_ref, *, t_len, tq):
    nb = t_len // LANES
    x = jnp.concatenate([lt_ref[:, LANES * k:LANES * (k + 1)] for k in range(nb)], axis=0)
    n = nb * N_HEADS
    r = lax.broadcasted_iota(jnp.int32, (LANES, LANES), 0)
    c = lax.broadcasted_iota(jnp.int32, (LANES, LANES), 1)
    incl = (r <= c).astype(BF16)
    y = sum(_nn(p, incl) for p in _split_bf16(x, 3))
    r = lax.broadcasted_iota(jnp.int32, (n, n), 0)
    c = lax.broadcasted_iota(jnp.int32, (n, n), 1)
    earlier = ((c // N_HEADS < r // N_HEADS) & (c % N_HEADS == r % N_HEADS)).astype(BF16)
    offs = sum(_nn(earlier, p) for p in _split_bf16(y, 3))[:, LANES - 1:LANES]
    z = (y + offs) * LOG2E
    zt = z.T
    per_q = tq // LANES
    for k in range(nb):
        ct_ref[k // per_q, :, LANES * (k % per_q):LANES * (k % per_q + 1)] = z[N_HEADS * k:N_HEADS * (k + 1), :]
        c_ref[LANES * k:LANES * (k + 1), :] = zt[:, N_HEADS * k:N_HEADS * (k + 1)]


def _cumsum(logf_t, *, tq):
    n_seq, _, t_len = logf_t.shape
    nq = t_len // tq
    return pl.pallas_call(
        functools.partial(_cumsum_kernel, t_len=t_len, tq=tq),
        grid=(n_seq,),
        in_specs=[pl.BlockSpec((None, N_HEADS, t_len), lambda b: (b, 0, 0))],
        out_specs=[pl.BlockSpec((t_len, N_HEADS), lambda b: (b, 0)),
                   pl.BlockSpec((None, nq, N_HEADS, tq), lambda b: (b, 0, 0, 0))],
        out_shape=[jax.ShapeDtypeStruct((n_seq * t_len, N_HEADS), F32),
                   jax.ShapeDtypeStruct((n_seq, nq, N_HEADS, tq), F32)],
        compiler_params=_params(1),
        name="logf_cumsum",
    )(logf_t)


def _head_pair_qt(qt_ref):
    qt = qt_ref[...]
    low = lax.broadcasted_iota(jnp.int32, qt.shape, 0) < HEAD_DIM
    zero = jnp.zeros_like(qt)
    return jnp.where(low, qt, zero), jnp.where(low, zero, qt)


def _head_column(c, head):
    hcol = lax.broadcasted_iota(jnp.int32, c.shape, 1)
    return jnp.sum(jnp.where(hcol == head, c, 0.0), axis=1, keepdims=True)


def _head_row(ct, head):
    hrow = lax.broadcasted_iota(jnp.int32, ct.shape, 0)
    return jnp.sum(jnp.where(hrow == head, ct, 0.0), axis=0, keepdims=True)


def _fox_kernel(qt_ref, k_ref, vt_ref, c_ref, ct_ref, o_ref, ck_s, s_s, *, tq):
    pair = pl.program_id(1)
    qi = pl.program_id(2)

    @pl.when(qi == 0)
    def _():
        c_all = c_ref[...]
        for e in (0, 1):
            ck_s[e] = jnp.broadcast_to(_head_column(c_all, 2 * pair + e), ck_s.shape[1:])

    qts = _head_pair_qt(qt_ref)
    cq_all = ct_ref[...]
    cq = [_head_row(cq_all, 2 * pair + e) for e in (0, 1)]
    key = lax.broadcasted_iota(jnp.int32, (tq, tq), 0)
    qry = lax.broadcasted_iota(jnp.int32, (tq, tq), 1)
    reps = tq // LANES
    n_full = qi // 2

    def logits(i, m, masked):
        js = (2 * i, 2 * i + 1)
        starts = [pl.multiple_of(j * tq, tq) for j in js]
        qk = [[_nn(k_ref[pl.ds(st, tq), :], qts[e]) for e in (0, 1)] for st in starts]
        m = list(m)
        for b, (j, st) in enumerate(zip(js, starts)):
            for e in (0, 1):
                ck = ck_s[e, pl.ds(st, tq), :]
                s = qk[b][e] + (cq[e] - jnp.concatenate([ck] * reps, axis=1))
                if masked:
                    s = jnp.where(key + (j - qi) * tq <= qry, s, NEG)
                s_s[e, j] = s
                m[e] = jnp.maximum(m[e], jnp.max(s, axis=0, keepdims=True))
        return tuple(m)

    m = lax.fori_loop(0, n_full, lambda i, m: logits(i, m, False), (jnp.full((1, tq), NEG, F32),) * 2)
    m = logits(n_full, m, True)

    def weigh(i, carry):
        js = (2 * i, 2 * i + 1)
        p = [[jnp.exp2(s_s[e, j] - m[e]) for e in (0, 1)] for j in js]
        pv = [[_nn(vt_ref[j, HEAD_DIM * e:HEAD_DIM * (e + 1), :], p[b][e].astype(BF16)) for e in (0, 1)]
              for b, j in enumerate(js)]
        return tuple((carry[e][0] + (jnp.sum(p[0][e], axis=0, keepdims=True) + jnp.sum(p[1][e], axis=0, keepdims=True)),
                      carry[e][1] + (pv[0][e] + pv[1][e])) for e in (0, 1))

    init = ((jnp.zeros((1, tq), F32), jnp.zeros((HEAD_DIM, tq), F32)),) * 2
    (l0, a0), (l1, a1) = lax.fori_loop(0, n_full + 1, weigh, init)
    o_ref[...] = jnp.concatenate([a0 / l0, a1 / l1], axis=0).T.astype(o_ref.dtype)


def _sb_kernel(qt_ref, k_ref, vt_ref, tri_ref, o_ref, lr_s, pos_s, tot_s, *, tq):
    qi = pl.program_id(2)
    qts = _head_pair_qt(qt_ref)
    key = lax.broadcasted_iota(jnp.int32, (tq, tq), 0)
    qry = lax.broadcasted_iota(jnp.int32, (tq, tq), 1)
    n_full = qi // 2

    def gates(i, masked):
        js = (2 * i, 2 * i + 1)
        z = [[_nn(k_ref[pl.ds(pl.multiple_of(j * tq, tq), tq), :], qts[e]) for e in (0, 1)] for j in js]
        for b, j in enumerate(js):
            for e in (0, 1):
                ze = z[b][e]
                lr = jnp.minimum(-ze, 0.0) - jnp.log2(1.0 + jnp.exp2(-jnp.abs(ze)))
                ls_pos = lr + ze
                if masked:
                    strict = key + (j - qi) * tq < qry
                    lr = jnp.where(strict, lr, 0.0)
                    ls_pos = jnp.where(strict, ls_pos, NEG)
                lr_s[e, j] = lr.astype(BF16)
                pos_s[e, j] = ls_pos
                tot_s[e, j] = jnp.sum(lr, axis=0, keepdims=True)

    def unmasked(i, carry):
        gates(i, False)
        return carry

    lax.fori_loop(0, n_full, unmasked, 0)
    gates(n_full, True)

    tri = tri_ref[...]

    def weigh(i, carry):
        js = (2 * (n_full - i) + 1, 2 * (n_full - i))
        between = [[_nn(tri, lr_s[e, j]) for e in (0, 1)] for j in js]
        rest = [(carry[e][0], carry[e][0] + tot_s[e, js[0]]) for e in (0, 1)]
        a = [[jnp.exp2(pos_s[e, j] + (between[b][e] + rest[e][b])).astype(BF16) for e in (0, 1)]
             for b, j in enumerate(js)]
        pv = [[_nn(vt_ref[j, HEAD_DIM * e:HEAD_DIM * (e + 1), :], a[b][e]) for e in (0, 1)]
              for b, j in enumerate(js)]
        return tuple((rest[e][1] + tot_s[e, js[1]], carry[e][1] + (pv[0][e] + pv[1][e])) for e in (0, 1))

    init = ((jnp.zeros((1, tq), F32), jnp.zeros((HEAD_DIM, tq), F32)),) * 2
    (_, a0), (_, a1) = lax.fori_loop(0, n_full + 1, weigh, init)
    o_ref[...] = jnp.concatenate([a0, a1], axis=0).T.astype(o_ref.dtype)


def _prompt_attention(kernel_fn, name, qt, k, vt16, extra, extra_specs, scratch, *, tq):
    n_seq, nq = vt16.shape[:2]
    assert nq % 2 == 0, "key blocks are consumed two at a time"
    t_len = nq * tq
    n_pair = WIDTH // LANES
    return pl.pallas_call(
        functools.partial(kernel_fn, tq=tq),
        grid=(n_seq, n_pair, nq),
        in_specs=[pl.BlockSpec((LANES, tq), lambda b, p, i: (p, b * nq + i)),
                  pl.BlockSpec((t_len, LANES), lambda b, p, i: (b, p)),
                  pl.BlockSpec((None, nq, LANES, tq), lambda b, p, i: (b, 0, p, 0))] + extra_specs,
        out_specs=pl.BlockSpec((tq, LANES), lambda b, p, i: (b * nq + i, p)),
        out_shape=jax.ShapeDtypeStruct(k.shape, BF16),
        scratch_shapes=scratch,
        compiler_params=_params(3),
        name=name,
    )(qt, k, vt16, *extra)


def _sample_attn_kernel(pt_ref, qa_ref, qb_ref, kan_ref, van_ref, kbn_ref, vbn_ref, lfn_ref,
                        tri_ref, hm_ref, *rest, n_pg, dec_b, dec_t):
    del pt_ref
    pg_ak, pg_av, pg_bk, pg_bv, pg_lf = (rest[n_pg * i:n_pg * (i + 1)] for i in range(5))
    oa_ref, ob_ref = rest[5 * n_pg:5 * n_pg + 2]
    qa_s, qb_s, acca_s, accb_s, m_s, l_s, dec_s, rest_s = rest[5 * n_pg + 2:]
    b = pl.program_id(0)
    j = pl.program_id(1)
    nr = dec_t * N_HEADS
    hm = hm_ref[...]

    def token_rows(ref):
        return [ref[pl.ds(t * dec_b + b, 1), :] for t in range(dec_t)]

    def bf16_exact(x):
        return x.astype(BF16).astype(F32)

    def lanes(xs):
        return xs[0] if len(xs) == 1 else jnp.concatenate(xs, axis=1)

    def update(kt_a, vt_a, kt_b, vt_b, lft, own):
        n = PAGE_SIZE * len(kt_a)
        sa = _nn(qa_s[...], lanes(kt_a))
        sb = _nn(qb_s[...], lanes(kt_b))
        lf = jnp.concatenate([lanes(lft)] * dec_t, axis=0)
        ls_neg, ls_pos = _log_sigmoid_pair(sb)
        if own:
            key = lax.broadcasted_iota(jnp.int32, (nr, n), 1)
            t_row = lax.broadcasted_iota(jnp.int32, (nr, n), 0) // N_HEADS
            valid_a = key <= t_row
            valid_b = key < t_row
            lr = jnp.where(valid_b, ls_neg, 0.0)
        else:
            lr = ls_neg
        stacked = jnp.concatenate(_split_bf16(lf, 2) + _split_bf16(lr, 2), axis=0)
        later = _nn(stacked, tri_ref[0:n, 0:n])
        later_a = later[0:nr] + later[nr:2 * nr]
        later_b = later[2 * nr:3 * nr] + later[3 * nr:4 * nr]

        if own:
            dec = -jnp.sum(jnp.where(key == t_row, later_a, 0.0), axis=1, keepdims=True)
            s = jnp.where(valid_a, sa + (later_a + dec), NEG)
        else:
            dec = dec_s[...]
            s = sa + (later_a + dec)
        m_old = m_s[...]
        m_new = jnp.maximum(m_old, jnp.max(s, axis=1, keepdims=True))
        alpha = jnp.exp(m_old - m_new)
        p = jnp.exp(s - m_new)
        a = jnp.exp(ls_pos + later_b + rest_s[...])
        if own:
            a = jnp.where(valid_b, a, 0.0)
        pv_a = _nt(bf16_exact(p), lanes(vt_a))
        pv_b = _nt(bf16_exact(a), lanes(vt_b))
        l_s[...] = alpha * l_s[...] + jnp.sum(p, axis=1, keepdims=True)
        acca_s[...] = alpha * acca_s[...] + pv_a
        m_s[...] = m_new
        dec_s[...] = dec + jnp.sum(lf, axis=1, keepdims=True)
        accb_s[...] = accb_s[...] + pv_b
        rest_s[...] = rest_s[...] + jnp.sum(lr, axis=1, keepdims=True)

    @pl.when(j == 0)
    def _():
        def block_diag_q(ref):
            return bf16_exact(jnp.concatenate(
                [jnp.broadcast_to(r, (N_HEADS, WIDTH)) * hm for r in token_rows(ref)], axis=0))

        def own_tile(ref):
            sub = lax.broadcasted_iota(jnp.int32, (SUBLANES, WIDTH), 0)
            top = jnp.zeros((SUBLANES, WIDTH), F32)
            for t, r in enumerate(token_rows(ref)):
                top = jnp.where(sub == t, jnp.broadcast_to(r, (SUBLANES, WIDTH)), top)
            return jnp.concatenate([top, jnp.zeros((PAGE_SIZE - SUBLANES, WIDTH), F32)], axis=0).T

        qa_s[...] = block_diag_q(qa_ref)
        qb_s[...] = block_diag_q(qb_ref)
        lane = lax.broadcasted_iota(jnp.int32, (N_HEADS, dec_b * dec_t), 1)
        pos = lax.broadcasted_iota(jnp.int32, (N_HEADS, PAGE_SIZE), 1)
        lfn = lfn_ref[...]
        lf0 = jnp.zeros((N_HEADS, PAGE_SIZE), F32)
        for t in range(dec_t):
            col = jnp.sum(jnp.where(lane == t * dec_b + b, lfn, 0.0), axis=1, keepdims=True)
            lf0 = jnp.where(pos == t, col, lf0)
        acca_s[...] = jnp.zeros((nr, WIDTH), F32)
        accb_s[...] = jnp.zeros((nr, WIDTH), F32)
        m_s[...] = jnp.full((nr, 1), NEG, F32)
        l_s[...] = jnp.zeros((nr, 1), F32)
        dec_s[...] = jnp.zeros((nr, 1), F32)
        rest_s[...] = jnp.zeros((nr, 1), F32)
        update([own_tile(kan_ref)], [own_tile(van_ref)], [own_tile(kbn_ref)], [own_tile(vbn_ref)], [lf0], True)

    @pl.when(j > 0)
    def _():
        for hi in range(n_pg, 0, -PAGES_PER_BLOCK):
            update(*[[pg[...] for pg in pgs[hi - PAGES_PER_BLOCK:hi]]
                     for pgs in (pg_ak, pg_av, pg_bk, pg_bv, pg_lf)], False)

    @pl.when(j == pl.num_programs(1) - 1)
    def _():
        oa = acca_s[...] / l_s[...]
        ob = accb_s[...]
        for t in range(dec_t):
            rows = slice(N_HEADS * t, N_HEADS * (t + 1))
            oa_ref[pl.ds(t * dec_b + b, 1), :] = jnp.sum(oa[rows] * hm, axis=0, keepdims=True)
            ob_ref[pl.ds(t * dec_b + b, 1), :] = jnp.sum(ob[rows] * hm, axis=0, keepdims=True)


def _sample_attention(layer, page_table, qa, qb, kan, van, kbn, vbn, lfn_t,
                      cache_a_kt, cache_a_vt, cache_a_lft, cache_b_kt, cache_b_vt, *, dec_b, dec_t):
    n_pg = PAGES_PER_STEP
    n_pages = page_table.shape[0] // dec_b
    nk = PAGES_PER_BLOCK * PAGE_SIZE
    nr = dec_t * N_HEADS
    m = dec_b * dec_t
    r = lax.broadcasted_iota(jnp.int32, (nk, nk), 0)
    c = lax.broadcasted_iota(jnp.int32, (nk, nk), 1)
    tri = (r > c).astype(BF16)
    hm = (lax.broadcasted_iota(jnp.int32, (N_HEADS, WIDTH), 1) // HEAD_DIM
          == lax.broadcasted_iota(jnp.int32, (N_HEADS, WIDTH), 0)).astype(F32)

    def page_spec(i, rows):
        def index(b, j, pt):
            pg = n_pages - jnp.maximum(j, 1) * n_pg + i
            return (layer, pt[b * n_pages + pg], 0, 0)
        return pl.BlockSpec((None, None, rows, PAGE_SIZE), index)

    full = lambda shape: pl.BlockSpec(shape, lambda b, j, pt: (0,) * len(shape))
    caches = (cache_a_kt, cache_a_vt, cache_b_kt, cache_b_vt, cache_a_lft)
    heights = (WIDTH, WIDTH, WIDTH, WIDTH, N_HEADS)
    page_specs = [page_spec(i, h) for h in heights for i in range(n_pg)]
    page_args = [cch for cch in caches for _ in range(n_pg)]
    grid_spec = pltpu.PrefetchScalarGridSpec(
        num_scalar_prefetch=1,
        grid=(dec_b, 1 + n_pages // n_pg),
        in_specs=[full((m, WIDTH))] * 6 + [full((N_HEADS, m)), full((nk, nk)), full((N_HEADS, WIDTH))]
        + page_specs,
        out_specs=[full((m, WIDTH)), full((m, WIDTH))],
        scratch_shapes=[pltpu.VMEM((nr, WIDTH), F32)] * 4 + [pltpu.VMEM((nr, 1), F32)] * 4,
    )
    return pl.pallas_call(
        functools.partial(_sample_attn_kernel, n_pg=n_pg, dec_b=dec_b, dec_t=dec_t),
        grid_spec=grid_spec,
        out_shape=[jax.ShapeDtypeStruct((m, WIDTH), F32)] * 2,
        compiler_params=_params(2),
        name="sample_attention",
    )(page_table, qa, qb, kan, van, kbn, vbn, lfn_t, tri, hm, *page_args)


def _merge_kernel(x_ref, oa_ref, ob_ref, gb_ref, u_ref, uprev_ref, st_ref, cw_ref, gpre_ref, gpost_ref,
                  wg_ref, bg_ref, wpa_ref, wpb_ref, wpc_ref, wo_ref, xo_ref, nb_ref, ext_s,
                  *, tm, shift, halo, tiles_per_seq):
    i = pl.program_id(0)
    x = x_ref[...]
    h = _rms(x, gpre_ref[...]).astype(BF16)
    u = u_ref[...]
    seq_start = i % tiles_per_seq == 0
    ext_s[0:halo, :] = jnp.where(seq_start, st_ref[0], uprev_ref[...])
    ext_s[halo:halo + tm, :] = u
    conv = (cw_ref[0:1, :] * ext_s[halo - 2 * shift:halo - 2 * shift + tm, :]
            + cw_ref[1:2, :] * ext_s[halo - shift:halo - shift + tm, :]
            + cw_ref[2:3, :] * u)
    yc = gb_ref[...].astype(F32) * conv
    mixed = jnp.zeros((tm, D_MODEL), F32)
    branches = ((oa_ref[...].astype(BF16), wpa_ref), (ob_ref[...].astype(BF16), wpb_ref), (yc.astype(BF16), wpc_ref))
    for br, (o, wp_ref) in enumerate(branches):
        cols = slice(D_MODEL * br, D_MODEL * (br + 1))
        gate = jax.nn.sigmoid(_nn(h, wg_ref[:, cols]) + bg_ref[:, cols])
        mixed = mixed + gate * _nn(o, wp_ref[...])
    y = _nn(mixed.astype(BF16), wo_ref[...])
    xo_ref[...] = x + _rms(y, gpost_ref[...])

    @pl.when(i % tiles_per_seq == tiles_per_seq - 1)
    def _():
        nb_ref[0] = ext_s[halo + tm - 2 * shift:halo + tm, :]


def _merge(x, oa, ob, gb, u, state_halo, conv_w, g_pre, g_post, w_gate, b_gate, wpa, wpb, wpc, wo,
           *, tm, shift, tiles_per_seq):
    m = x.shape[0]
    halo = state_halo.shape[1]
    n_groups = m // (tm * tiles_per_seq)
    row = lambda i: (i, 0)
    const = lambda i: (0, 0)
    wide = pl.BlockSpec((tm, WIDTH), row)
    return pl.pallas_call(
        functools.partial(_merge_kernel, tm=tm, shift=shift, halo=halo, tiles_per_seq=tiles_per_seq),
        grid=(m // tm,),
        in_specs=[pl.BlockSpec((tm, D_MODEL), row), wide, wide, wide, wide,
                  pl.BlockSpec((halo, WIDTH), lambda i: (jnp.maximum(i * (tm // halo) - 1, 0), 0)),
                  pl.BlockSpec((1, halo, WIDTH), lambda i: (i // tiles_per_seq, 0, 0)),
                  pl.BlockSpec((CONV_W, WIDTH), const),
                  pl.BlockSpec((1, D_MODEL), const), pl.BlockSpec((1, D_MODEL), const),
                  pl.BlockSpec(w_gate.shape, const), pl.BlockSpec((1, N_BRANCH * D_MODEL), const),
                  pl.BlockSpec(wpa.shape, const), pl.BlockSpec(wpb.shape, const),
                  pl.BlockSpec(wpc.shape, const), pl.BlockSpec(wo.shape, const)],
        out_specs=[pl.BlockSpec((tm, D_MODEL), row),
                   pl.BlockSpec((1, 2 * shift, WIDTH), lambda i: (i // tiles_per_seq, 0, 0))],
        out_shape=[jax.ShapeDtypeStruct((m, D_MODEL), F32),
                   jax.ShapeDtypeStruct((n_groups, 2 * shift, WIDTH), F32)],
        scratch_shapes=[pltpu.VMEM((halo + tm, WIDTH), F32)],
        compiler_params=_params(1),
        name="merge",
    )(x, oa, ob, gb, u, u, state_halo, conv_w, g_pre, g_post, w_gate, b_gate, wpa, wpb, wpc, wo)


def _ffn_kernel(x_ref, gpre_ref, gpost_ref, wg_ref, wu_ref, wd_ref, o_ref, h_s, acc_s):
    j = pl.program_id(1)

    @pl.when(j == 0)
    def _():
        h_s[...] = _rms(x_ref[...], gpre_ref[...]).astype(BF16)
        acc_s[...] = jnp.zeros_like(acc_s)

    h = h_s[...]
    g = _nn(h, wg_ref[...])
    up = _nn(h, wu_ref[...])
    act = (g * jax.nn.sigmoid(g) * up).astype(BF16)
    acc_s[...] += _nn(act, wd_ref[...])

    @pl.when(j == pl.num_programs(1) - 1)
    def _():
        o_ref[...] = x_ref[...] + _rms(acc_s[...], gpost_ref[...])


def _ffn(x, g_pre, g_post, wg, wu, wd, *, tm, tf):
    m = x.shape[0]
    row = lambda i, j: (i, 0)
    const = lambda i, j: (0, 0)
    return pl.pallas_call(
        _ffn_kernel,
        grid=(m // tm, D_FF // tf),
        in_specs=[pl.BlockSpec((tm, D_MODEL), row),
                  pl.BlockSpec((1, D_MODEL), const), pl.BlockSpec((1, D_MODEL), const),
                  pl.BlockSpec((D_MODEL, tf), lambda i, j: (0, j)),
                  pl.BlockSpec((D_MODEL, tf), lambda i, j: (0, j)),
                  pl.BlockSpec((tf, D_MODEL), lambda i, j: (j, 0))],
        out_specs=pl.BlockSpec((tm, D_MODEL), row),
        out_shape=jax.ShapeDtypeStruct((m, D_MODEL), F32),
        scratch_shapes=[pltpu.VMEM((tm, D_MODEL), BF16), pltpu.VMEM((tm, D_MODEL), F32)],
        compiler_params=_params(2),
        name="ffn",
    )(x, g_pre, g_post, wg, wu, wd)


def kernel(x_prompt, x_sample, cache_a_k, cache_a_v, cache_a_logf, cache_b_k, cache_b_v, state_conv, page_table, w_in, b_f, conv_w, w_proj_a, w_proj_b, w_proj_c, w_gate, b_gate, w_o, g_mix_pre, g_mix_post, g_ffn_pre, g_ffn_post, w_ffn_gate, w_ffn_up, w_ffn_down):
    n_seq, t_len, _ = x_prompt.shape
    dec_b, dec_t, _ = x_sample.shape
    depth = w_in.shape[0]
    n_pool = cache_a_k.shape[1]
    mp = n_seq * t_len
    ms = dec_b * dec_t
    tq = ATT_BLOCK
    tm_p = 512

    xp = x_prompt.reshape(mp, D_MODEL)
    xs = jnp.transpose(x_sample, (1, 0, 2)).reshape(ms, D_MODEL)
    pt = page_table.reshape(-1)
    feature_major = lambda c: jnp.transpose(c, (0, 1, 3, 4, 2)).reshape(depth, n_pool, WIDTH, PAGE_SIZE)
    caches = (feature_major(cache_a_k), feature_major(cache_a_v), jnp.transpose(cache_a_logf, (0, 1, 3, 2)),
              feature_major(cache_b_k), feature_major(cache_b_v))
    r = lax.broadcasted_iota(jnp.int32, (tq, tq), 0)
    c = lax.broadcasted_iota(jnp.int32, (tq, tq), 1)
    tri = (c > r).astype(BF16)
    prompt_halo = jnp.zeros((n_seq, SUBLANES, WIDTH), F32)

    rows_p, rows_s = [], []
    qkv = 3 * WIDTH
    for l in range(depth):
        wl = w_in[l]
        wt = jnp.concatenate([wl[:, 0:qkv], wl[:, qkv + N_HEADS:2 * qkv + N_HEADS]], axis=1).T.astype(BF16)
        wc = wl[:, 2 * qkv + N_HEADS:].astype(BF16)
        w_f = wl[:, qkv:qkv + N_HEADS]
        wf = jnp.pad(w_f, ((0, 0), (0, LANES - N_HEADS))).astype(BF16)
        wft = jnp.pad(w_f.T, ((0, 2 * SUBLANES - N_HEADS), (0, 0))).astype(BF16)
        g_pre, g_post = g_mix_pre[l].reshape(1, -1), g_mix_post[l].reshape(1, -1)
        f_pre, f_post = g_ffn_pre[l].reshape(1, -1), g_ffn_post[l].reshape(1, -1)
        merge_w = (conv_w[l], g_pre, g_post, w_gate[l].astype(BF16), b_gate[l].reshape(1, -1),
                   w_proj_a[l].astype(BF16), w_proj_b[l].astype(BF16), w_proj_c[l].astype(BF16),
                   w_o[l].astype(BF16))
        ffn_w = (f_pre, f_post, w_ffn_gate[l].astype(BF16), w_ffn_up[l].astype(BF16),
                 w_ffn_down[l].astype(BF16))

        (qat, qbt, ka16, kb16, kat, vat, kbt, vbt, vat16, vbt16, gb, u, logf_t) = _in_proj_prompt(
            xp, g_pre, wt, wc, wft, b_f[l], n_seq=n_seq, t_len=t_len, tm=tm_p, tq=tq)
        cum, cum_t = _cumsum(logf_t, tq=tq)
        nq = t_len // tq
        oa = _prompt_attention(
            _fox_kernel, "fox_attention", qat, ka16, vat16, (cum, cum_t),
            [pl.BlockSpec((t_len, N_HEADS), lambda b, p, i: (b, 0)),
             pl.BlockSpec((None, None, N_HEADS, tq), lambda b, p, i: (b, i, 0, 0))],
            [pltpu.VMEM((2, t_len, LANES), F32), pltpu.VMEM((2, nq, tq, tq), F32)], tq=tq)
        ob = _prompt_attention(
            _sb_kernel, "sb_attention", qbt, kb16, vbt16, (tri,),
            [pl.BlockSpec((tq, tq), lambda b, p, i: (0, 0))],
            [pltpu.VMEM((2, nq, tq, tq), BF16), pltpu.VMEM((2, nq, tq, tq), F32),
             pltpu.VMEM((2, nq, 1, tq), F32)], tq=tq)
        xp, nb_p = _merge(xp, oa, ob, gb, u, prompt_halo, *merge_w,
                          tm=tm_p, shift=1, tiles_per_seq=t_len // tm_p)
        xp = _ffn(xp, *ffn_w, tm=1024, tf=256)
        rows_p.append((kat, vat, logf_t, kbt, vbt, nb_p))

        (qa, ka, va, qb, kb, vb, gb, u, logf, logf_t) = _in_proj_sample(
            xs, g_pre, wt, wc, wf, wft, b_f[l])
        oa, ob = _sample_attention(l, pt, qa, qb, ka, va, kb, vb, logf_t, *caches,
                                   dec_b=dec_b, dec_t=dec_t)
        state = jnp.transpose(state_conv[l], (1, 0, 2)).reshape(1, (CONV_W - 1) * dec_b, WIDTH)
        xs, nb_s = _merge(xs, oa, ob, gb, u, state, *merge_w, tm=ms, shift=dec_b, tiles_per_seq=1)
        xs = _ffn(xs, *ffn_w, tm=ms, tf=256)
        rows_s.append((ka, va, logf, kb, vb, nb_s))

    stack = lambda rows, i: jnp.stack([r[i] for r in rows], axis=0)

    def new_kv_p(i):
        a = stack(rows_p, i).reshape(depth, n_seq, N_HEADS, HEAD_DIM, t_len)
        return jnp.transpose(a, (0, 1, 4, 2, 3))

    def stack_s(i, tail):
        a = stack(rows_s, i).reshape(depth, -1, dec_b, *tail)
        return jnp.swapaxes(a, 1, 2)

    heads = (N_HEADS, HEAD_DIM)
    return (xp.reshape(n_seq, t_len, D_MODEL),
            jnp.transpose(xs.reshape(dec_t, dec_b, D_MODEL), (1, 0, 2)),
            new_kv_p(0), new_kv_p(1), jnp.transpose(stack(rows_p, 2), (0, 1, 3, 2)),
            new_kv_p(3), new_kv_p(4), stack(rows_p, 5),
            stack_s(0, heads), stack_s(1, heads), stack_s(2, (N_HEADS,)),
            stack_s(3, heads), stack_s(4, heads), stack_s(5, (WIDTH,)))
```

```python
import functools

import jax
import jax.numpy as jnp
from jax import lax
from jax.experimental import pallas as pl
from jax.experimental.pallas import tpu as pltpu

D_MODEL = 1024
HEAD_DIM = 64
N_HEADS = 8
WIDTH = N_HEADS * HEAD_DIM
CONV_W = 3
D_FF = 2816
N_BRANCH = 3
PAGE_SIZE = 128
RMS_EPS = 1e-6
SCALE = HEAD_DIM ** -0.5
LOG2E = 1.4426950408889634

LANES = 128
SUBLANES = 8
VMEM_LIMIT_BYTES = 52 * 1024 * 1024

BF16 = jnp.bfloat16
F32 = jnp.float32
NEG = -1e30
SIGN_BIT = -2 ** 31

ATT_BLOCK = 256
PAGES_PER_STEP = 8
PAGES_PER_BLOCK = 4


def _nn(a, b):
    return jnp.dot(a, b, preferred_element_type=F32)


def _nt(a, b):
    return lax.dot_general(a, b, (((1,), (1,)), ((), ())), preferred_element_type=F32)


def _rms(x, g):
    ms = jnp.mean(x * x, axis=-1, keepdims=True)
    return x * lax.rsqrt(ms + RMS_EPS) * g


def _log_sigmoid(x):
    return jnp.minimum(x, 0.0) - jnp.log1p(jnp.exp(-jnp.abs(x)))


def _log_sigmoid_pair(z):
    soft = jnp.log1p(jnp.exp(-jnp.abs(z)))
    return jnp.minimum(-z, 0.0) - soft, jnp.minimum(z, 0.0) - soft


def _split_bf16(x, n):
    parts = []
    r = x
    for i in range(n):
        p = r.astype(BF16)
        parts.append(p)
        if i + 1 < n:
            r = r - p.astype(F32)
    return parts


def _params(n_axes):
    return pltpu.CompilerParams(dimension_semantics=("arbitrary",) * n_axes,
                                vmem_limit_bytes=VMEM_LIMIT_BYTES)


def _in_proj_prompt_kernel(x_ref, g_ref, wt_ref, wc_ref, wft_ref, bft_ref, *refs, tq, n_carried):
    (qat_ref, qbt_ref, ka_ref, kb_ref, kat_ref, vat_ref, kbt_ref, vbt_ref,
     vat16_ref, vbt16_ref, gb_ref, u_ref, lft_ref) = refs[n_carried:]
    h = _rms(x_ref[...], g_ref[...]).astype(BF16)
    tm = h.shape[0]
    rows = lambda k: wt_ref[WIDTH * k:WIDTH * (k + 1), :]
    for q_ref, kt_ref, k_ref, vt_ref, vt16_ref, base in ((qat_ref, kat_ref, ka_ref, vat_ref, vat16_ref, 0),
                                                         (qbt_ref, kbt_ref, kb_ref, vbt_ref, vbt16_ref, 3)):
        q_ref[...] = (_nt(rows(base), h) * (SCALE * LOG2E)).astype(BF16)
        kt_ref[...] = _nt(rows(base + 1), h)
        k_ref[...] = _nt(h, rows(base + 1)).astype(BF16)
        vt = _nt(rows(base + 2), h)
        vt_ref[...] = vt
        for c in range(tm // tq):
            vt16_ref[c] = vt[:, tq * c:tq * (c + 1)].astype(BF16)
    gb_ref[...] = _nn(h, wc_ref[:, 0:WIDTH]).astype(BF16)
    u_ref[...] = _nn(h, wc_ref[:, WIDTH:2 * WIDTH]) * _nn(h, wc_ref[:, 2 * WIDTH:3 * WIDTH])
    ft = _nt(wft_ref[...], h)[:N_HEADS, :]
    lft_ref[...] = _log_sigmoid(ft + bft_ref[...])


N_NEW_KV = 4
FIRST_NEW_KV_OUT = 4


def _in_proj_prompt(layer, depth, carried, x, g, wt, wc, wft, b_f, *, n_seq, t_len, tm, tq):
    m = x.shape[0]
    tps = t_len // tm
    nq = t_len // tq
    row = lambda i: (i, 0)
    const = lambda i: (0, 0)
    col = lambda i: (0, i)
    tok = lambda dt: (jax.ShapeDtypeStruct((m, WIDTH), dt), pl.BlockSpec((tm, WIDTH), row))
    feat = (jax.ShapeDtypeStruct((WIDTH, m), BF16), pl.BlockSpec((WIDTH, tm), col))
    new_t = (jax.ShapeDtypeStruct((depth, n_seq, WIDTH, t_len), F32),
             pl.BlockSpec((None, None, WIDTH, tm), lambda i: (layer, i // tps, 0, i % tps)))
    blocked = (jax.ShapeDtypeStruct((n_seq, nq, WIDTH, tq), BF16),
               pl.BlockSpec((None, tm // tq, WIDTH, tq), lambda i: (i // tps, i % tps, 0, 0)))
    outs = [feat, feat, tok(BF16), tok(BF16)] + [new_t] * N_NEW_KV + [blocked, blocked, tok(BF16), tok(F32),
            (jax.ShapeDtypeStruct((n_seq, N_HEADS, t_len), F32),
             pl.BlockSpec((None, N_HEADS, tm), lambda i: (i // tps, 0, i % tps)))]
    in_specs = [pl.BlockSpec((tm, D_MODEL), row),
                pl.BlockSpec((1, D_MODEL), const),
                pl.BlockSpec(wt.shape, const),
                pl.BlockSpec(wc.shape, const),
                pl.BlockSpec(wft.shape, const),
                pl.BlockSpec((N_HEADS, 1), const)]
    assert len(carried) in (0, N_NEW_KV)
    aliases = {len(in_specs) + k: FIRST_NEW_KV_OUT + k for k in range(len(carried))}
    return pl.pallas_call(
        functools.partial(_in_proj_prompt_kernel, tq=tq, n_carried=len(carried)),
        grid=(m // tm,),
        in_specs=in_specs + [pl.BlockSpec(memory_space=pl.ANY)] * len(carried),
        out_specs=[o[1] for o in outs],
        out_shape=[o[0] for o in outs],
        input_output_aliases=aliases,
        compiler_params=_params(1),
        name="in_proj_prompt",
    )(x, g, wt, wc, wft, b_f.reshape(N_HEADS, 1), *carried)


def _in_proj_sample_kernel(x_ref, g_ref, wt_ref, wc_ref, wf_ref, wft_ref, bf_ref, bft_ref,
                           qa_ref, ka_ref, va_ref, qb_ref, kb_ref, vb_ref, gb_ref, u_ref, lf_ref, lft_ref):
    h = _rms(x_ref[...], g_ref[...]).astype(BF16)
    for k, o_ref in enumerate((qa_ref, ka_ref, va_ref, qb_ref, kb_ref, vb_ref)):
        y = _nt(h, wt_ref[WIDTH * k:WIDTH * (k + 1), :])
        o_ref[...] = y * SCALE if k % 3 == 0 else y
    gb_ref[...] = _nn(h, wc_ref[:, 0:WIDTH]).astype(BF16)
    u_ref[...] = _nn(h, wc_ref[:, WIDTH:2 * WIDTH]) * _nn(h, wc_ref[:, 2 * WIDTH:3 * WIDTH])
    f = _nn(h, wf_ref[...])[:, :N_HEADS]
    lf_ref[...] = _log_sigmoid(f + bf_ref[...])
    ft = _nt(wft_ref[...], h)[:N_HEADS, :]
    lft_ref[...] = _log_sigmoid(ft + bft_ref[...])


def _in_proj_sample(x, g, wt, wc, wf, wft, b_f):
    m = x.shape[0]
    full = lambda shape: pl.BlockSpec(shape, lambda i: (0,) * len(shape))
    tok = lambda dt: jax.ShapeDtypeStruct((m, WIDTH), dt)
    out_shape = [tok(F32)] * 6 + [tok(BF16), tok(F32),
                                  jax.ShapeDtypeStruct((m, N_HEADS), F32), jax.ShapeDtypeStruct((N_HEADS, m), F32)]
    args = (x, g, wt, wc, wf, wft, b_f.reshape(1, N_HEADS), b_f.reshape(N_HEADS, 1))
    return pl.pallas_call(
        _in_proj_sample_kernel,
        grid=(1,),
        in_specs=[full(a.shape) for a in args],
        out_specs=[full(s.shape) for s in out_shape],
        out_shape=out_shape,
        compiler_params=_params(1),
        name="in_proj_sample",
    )(*args)


def _cumsum_kernel(lt_ref, c_ref, ct_ref, *, t_len, tq):
    nb = t_len // LANES
    x = jnp.concatenate([lt_ref[:, LANES * k:LANES * (k + 1)] for k in range(nb)], axis=0)
    n = nb * N_HEADS
    r = lax.broadcasted_iota(jnp.int32, (LANES, LANES), 0)
    c = lax.broadcasted_iota(jnp.int32, (LANES, LANES), 1)
    incl = (r <= c).astype(BF16)
    y = sum(_nn(p, incl) for p in _split_bf16(x, 3))
    r = lax.broadcasted_iota(jnp.int32, (n, n), 0)
    c = lax.broadcasted_iota(jnp.int32, (n, n), 1)
    earlier = ((c // N_HEADS < r // N_HEADS) & (c % N_HEADS == r % N_HEADS)).astype(BF16)
    offs = sum(_nn(earlier, p) for p in _split_bf16(y, 3))[:, LANES - 1:LANES]
    z = (y + offs) * LOG2E
    zt = z.T
    per_q = tq // LANES
    for k in range(nb):
        ct_ref[k // per_q, :, LANES * (k % per_q):LANES * (k % per_q + 1)] = z[N_HEADS * k:N_HEADS * (k + 1), :]
        c_ref[LANES * k:LANES * (k + 1), :] = zt[:, N_HEADS * k:N_HEADS * (k + 1)]


def _cumsum(logf_t, *, tq):
    n_seq, _, t_len = logf_t.shape
    nq = t_len // tq
    return pl.pallas_call(
        functools.partial(_cumsum_kernel, t_len=t_len, tq=tq),
        grid=(n_seq,),
        in_specs=[pl.BlockSpec((None, N_HEADS, t_len), lambda b: (b, 0, 0))],
        out_specs=[pl.BlockSpec((t_len, N_HEADS), lambda b: (b, 0)),
                   pl.BlockSpec((None, nq, N_HEADS, tq), lambda b: (b, 0, 0, 0))],
        out_shape=[jax.ShapeDtypeStruct((n_seq * t_len, N_HEADS), F32),
                   jax.ShapeDtypeStruct((n_seq, nq, N_HEADS, tq), F32)],
        compiler_params=_params(1),
        name="logf_cumsum",
    )(logf_t)


def _head_pair_qt(qt_ref):
    qt = qt_ref[...]
    low = lax.broadcasted_iota(jnp.int32, qt.shape, 0) < HEAD_DIM
    zero = jnp.zeros_like(qt)
    return jnp.where(low, qt, zero), jnp.where(low, zero, qt)


def _head_column(c, head):
    hcol = lax.broadcasted_iota(jnp.int32, c.shape, 1)
    return jnp.sum(jnp.where(hcol == head, c, 0.0), axis=1, keepdims=True)


def _head_row(ct, head):
    hrow = lax.broadcasted_iota(jnp.int32, ct.shape, 0)
    return jnp.sum(jnp.where(hrow == head, ct, 0.0), axis=0, keepdims=True)


def _fox_kernel(qt_ref, k_ref, vt_ref, c_ref, ct_ref, o_ref, ck_s, s_s, *, tq):
    pair = pl.program_id(1)
    qi = pl.program_id(2)

    @pl.when(qi == 0)
    def _():
        c_all = c_ref[...]
        for e in (0, 1):
            ck_s[e] = jnp.broadcast_to(_head_column(c_all, 2 * pair + e), ck_s.shape[1:])

    qts = _head_pair_qt(qt_ref)
    cq_all = ct_ref[...]
    cq = [_head_row(cq_all, 2 * pair + e) for e in (0, 1)]
    key = lax.broadcasted_iota(jnp.int32, (tq, tq), 0)
    qry = lax.broadcasted_iota(jnp.int32, (tq, tq), 1)
    reps = tq // LANES
    n_full = qi // 2

    def logits(js, m, masked):
        starts = [pl.multiple_of(j * tq, tq) for j in js]
        qk = [[_nn(k_ref[pl.ds(st, tq), :], qts[e]) for e in (0, 1)] for st in starts]
        m = list(m)
        for b, (j, st) in enumerate(zip(js, starts)):
            for e in (0, 1):
                ck = ck_s[e, pl.ds(st, tq), :]
                s = qk[b][e] + (cq[e] - jnp.concatenate([ck] * reps, axis=1))
                if masked:
                    s = jnp.where(key + (j - qi) * tq <= qry, s, NEG)
                s_s[e, j] = s
                m[e] = jnp.maximum(m[e], jnp.max(s, axis=0, keepdims=True))
        return tuple(m)

    even = qi % 2 == 0
    m = lax.fori_loop(0, n_full, lambda i, m: logits((2 * i, 2 * i + 1), m, False),
                      (jnp.full((1, tq), NEG, F32),) * 2)
    m = lax.cond(even, lambda m: logits((qi,), m, True), lambda m: logits((qi - 1, qi), m, True), m)

    ones = jnp.ones((2 * SUBLANES, tq), BF16)

    def weigh(js, acc):
        p = [[jnp.exp2(s_s[e, j] - m[e]).astype(BF16) for e in (0, 1)] for j in js]
        pv = [[_nn(jnp.concatenate([vt_ref[j, HEAD_DIM * e:HEAD_DIM * (e + 1), :], ones], axis=0), p[b][e])
               for e in (0, 1)] for b, j in enumerate(js)]
        return tuple(acc[e] + sum(x[e] for x in pv) for e in (0, 1))

    acc = lax.fori_loop(0, n_full, lambda i, acc: weigh((2 * i, 2 * i + 1), acc),
                        (jnp.zeros((HEAD_DIM + 2 * SUBLANES, tq), F32),) * 2)
    acc = lax.cond(even, lambda acc: weigh((qi,), acc), lambda acc: weigh((qi - 1, qi), acc), acc)
    out = [acc[e][0:HEAD_DIM] / acc[e][HEAD_DIM:HEAD_DIM + 1] for e in (0, 1)]
    o_ref[...] = jnp.concatenate(out, axis=0).T.astype(o_ref.dtype)


def _sb_kernel(qt_ref, k_ref, vt_ref, tri_ref, o_ref, lr_s, pos_s, *, tq):
    qi = pl.program_id(2)
    nqts = [-qt for qt in _head_pair_qt(qt_ref)]
    key = lax.broadcasted_iota(jnp.int32, (tq, tq), 0)
    qry = lax.broadcasted_iota(jnp.int32, (tq, tq), 1)
    n_full = qi // 2

    def gates(js, masked):
        nz = [[_nn(k_ref[pl.ds(pl.multiple_of(j * tq, tq), tq), :], nqts[e]) for e in (0, 1)] for j in js]
        for b, j in enumerate(js):
            for e in (0, 1):
                n = nz[b][e]
                neg_abs = lax.bitcast_convert_type(lax.bitcast_convert_type(n, jnp.int32) | SIGN_BIT, F32)
                lr = jnp.minimum(n, 0.0) - jnp.log2(1.0 + jnp.exp2(neg_abs))
                ls_pos = lr - n
                if masked:
                    strict = key + (j - qi) * tq < qry
                    lr = jnp.where(strict, lr, 0.0)
                    ls_pos = jnp.where(strict, ls_pos, NEG)
                lr_s[e, j] = lr.astype(BF16)
                pos_s[e, j] = ls_pos

    def unmasked(i, carry):
        gates((2 * i, 2 * i + 1), False)
        return carry

    even = qi % 2 == 0
    lax.fori_loop(0, n_full, unmasked, 0)

    @pl.when(even)
    def _():
        gates((qi,), True)

    @pl.when(jnp.logical_not(even))
    def _():
        gates((qi - 1, qi), True)

    tri = tri_ref[...]

    def weigh(js, carry):
        sums = [[_nn(tri, lr_s[e, j]) for e in (0, 1)] for j in js]
        rest =[[carry[e][0]] for e in (0, 1)]
        for b in range(len(js)):
            for e in (0, 1):
                rest[e].append(rest[e][b] + sums[b][e][tq:tq + 1])
        a = [[jnp.exp2(pos_s[e, j] + (sums[b][e][0:tq] + rest[e][b])).astype(BF16) for e in (0, 1)]
             for b, j in enumerate(js)]
        pv = [[_nn(vt_ref[j, HEAD_DIM * e:HEAD_DIM * (e + 1), :], a[b][e]) for e in (0, 1)]
              for b, j in enumerate(js)]
        return tuple((rest[e][-1], carry[e][1] + sum(x[e] for x in pv)) for e in (0, 1))

    init = ((jnp.zeros((1, tq), F32), jnp.zeros((HEAD_DIM, tq), F32)),) * 2
    carry = lax.cond(even, lambda c: weigh((qi,), c), lambda c: weigh((qi, qi - 1), c), init)
    top = 2 * n_full - 1
    (_, a0), (_, a1) = lax.fori_loop(0, n_full, lambda i, c: weigh((top - 2 * i, top - 2 * i - 1), c), carry)
    o_ref[...] = jnp.concatenate([a0, a1], axis=0).T.astype(o_ref.dtype)


def _prompt_attention(kernel_fn, name, qt, k, vt16, extra, extra_specs, scratch, *, tq):
    n_seq, nq = vt16.shape[:2]
    t_len = nq * tq
    n_pair = WIDTH // LANES
    return pl.pallas_call(
        functools.partial(kernel_fn, tq=tq),
        grid=(n_seq, n_pair, nq),
        in_specs=[pl.BlockSpec((LANES, tq), lambda b, p, i: (p, b * nq + i)),
                  pl.BlockSpec((t_len, LANES), lambda b, p, i: (b, p)),
                  pl.BlockSpec((None, nq, LANES, tq), lambda b, p, i: (b, 0, p, 0))] + extra_specs,
        out_specs=pl.BlockSpec((tq, LANES), lambda b, p, i: (b * nq + i, p)),
        out_shape=jax.ShapeDtypeStruct(k.shape, BF16),
        scratch_shapes=scratch,
        compiler_params=_params(3),
        name=name,
    )(qt, k, vt16, *extra)


def _sample_attn_kernel(pt_ref, qa_ref, qb_ref, kan_ref, van_ref, kbn_ref, vbn_ref, lfn_ref,
                        tri_ref, hm_ref, *rest, n_pg, dec_b, dec_t):
    del pt_ref
    pg_ak, pg_av, pg_bk, pg_bv, pg_lf = (rest[n_pg * i:n_pg * (i + 1)] for i in range(5))
    oa_ref, ob_ref = rest[5 * n_pg:5 * n_pg + 2]
    qa_s, qb_s, acca_s, accb_s, m_s, l_s, dec_s, rest_s = rest[5 * n_pg + 2:]
    b = pl.program_id(0)
    j = pl.program_id(1)
    nr = dec_t * N_HEADS
    hm = hm_ref[...]

    def token_rows(ref):
        return [ref[pl.ds(t * dec_b + b, 1), :] for t in range(dec_t)]

    def bf16_exact(x):
        return x.astype(BF16).astype(F32)

    def lanes(xs):
        return xs[0] if len(xs) == 1 else jnp.concatenate(xs, axis=1)

    def update(blocks, own):
        n = PAGE_SIZE * len(blocks[0][0])
        rowsum = lambda x: jnp.sum(x, axis=1, keepdims=True)
        sa = [_nn(qa_s[...], lanes(blk[0])) for blk in blocks]
        sb = [_nn(qb_s[...], lanes(blk[2])) for blk in blocks]
        lf = [jnp.concatenate([lanes(blk[4])] * dec_t, axis=0) for blk in blocks]
        ls = [_log_sigmoid_pair(x) for x in sb]
        if own:
            key = lax.broadcasted_iota(jnp.int32, (nr, n), 1)
            t_row = lax.broadcasted_iota(jnp.int32, (nr, n), 0) // N_HEADS
            valid_a = key <= t_row
            valid_b = key < t_row
            lr = [jnp.where(valid_b, x[0], 0.0) for x in ls]
        else:
            lr = [x[0] for x in ls]
        tri = tri_ref[0:n, 0:n]
        later = [_nn(jnp.concatenate(_split_bf16(f, 2) + _split_bf16(r, 2), axis=0), tri)
                 for f, r in zip(lf, lr)]
        later_a = [x[0:nr] + x[nr:2 * nr] for x in later]
        later_b = [x[2 * nr:3 * nr] + x[3 * nr:4 * nr] for x in later]

        dec = [-rowsum(jnp.where(key == t_row, later_a[0], 0.0)) if own else dec_s[...]]
        rest = [rest_s[...]]
        for f, r in zip(lf, lr):
            dec.append(dec[-1] + rowsum(f))
            rest.append(rest[-1] + rowsum(r))
        s = [x + (la + d) for x, la, d in zip(sa, later_a, dec)]
        if own:
            s = [jnp.where(valid_a, x, NEG) for x in s]
        m_old = m_s[...]
        m_new = functools.reduce(jnp.maximum, [m_old] + [jnp.max(x, axis=1, keepdims=True) for x in s])
        alpha = jnp.exp(m_old - m_new)
        p = [jnp.exp(x - m_new) for x in s]
        a = [jnp.exp(x[1] + lb + r) for x, lb, r in zip(ls, later_b, rest)]
        if own:
            a = [jnp.where(valid_b, x, 0.0) for x in a]
        pv_a = sum(_nt(bf16_exact(x), lanes(blk[1])) for x, blk in zip(p, blocks))
        pv_b = sum(_nt(bf16_exact(x), lanes(blk[3])) for x, blk in zip(a, blocks))
        l_s[...] = alpha * l_s[...] + sum(rowsum(x) for x in p)
        acca_s[...] = alpha * acca_s[...] + pv_a
        m_s[...] = m_new
        dec_s[...] = dec[-1]
        accb_s[...] = accb_s[...] + pv_b
        rest_s[...] = rest[-1]

    @pl.when(j == 0)
    def _():
        def block_diag_q(ref):
            return bf16_exact(jnp.concatenate(
                [jnp.broadcast_to(r, (N_HEADS, WIDTH)) * hm for r in token_rows(ref)], axis=0))

        def own_tile(ref):
            sub = lax.broadcasted_iota(jnp.int32, (SUBLANES, WIDTH), 0)
            top = jnp.zeros((SUBLANES, WIDTH), F32)
            for t, r in enumerate(token_rows(ref)):
                top = jnp.where(sub == t, jnp.broadcast_to(r, (SUBLANES, WIDTH)), top)
            return jnp.concatenate([top, jnp.zeros((PAGE_SIZE - SUBLANES, WIDTH), F32)], axis=0).T

        qa_s[...] = block_diag_q(qa_ref)
        qb_s[...] = block_diag_q(qb_ref)
        lane = lax.broadcasted_iota(jnp.int32, (N_HEADS, dec_b * dec_t), 1)
        pos = lax.broadcasted_iota(jnp.int32, (N_HEADS, PAGE_SIZE), 1)
        lfn = lfn_ref[...]
        lf0 = jnp.zeros((N_HEADS, PAGE_SIZE), F32)
        for t in range(dec_t):
            col = jnp.sum(jnp.where(lane == t * dec_b + b, lfn, 0.0), axis=1, keepdims=True)
            lf0 = jnp.where(pos == t, col, lf0)
        acca_s[...] = jnp.zeros((nr, WIDTH), F32)
        accb_s[...] = jnp.zeros((nr, WIDTH), F32)
        m_s[...] = jnp.full((nr, 1), NEG, F32)
        l_s[...] = jnp.zeros((nr, 1), F32)
        dec_s[...] = jnp.zeros((nr, 1), F32)
        rest_s[...] = jnp.zeros((nr, 1), F32)
        update([([own_tile(kan_ref)], [own_tile(van_ref)], [own_tile(kbn_ref)], [own_tile(vbn_ref)], [lf0])], True)

    @pl.when(j > 0)
    def _():
        update([[[pg[...] for pg in pgs[hi - PAGES_PER_BLOCK:hi]] for pgs in (pg_ak, pg_av, pg_bk, pg_bv, pg_lf)]
                for hi in range(n_pg, 0, -PAGES_PER_BLOCK)], False)

    @pl.when(j == pl.num_programs(1) - 1)
    def _():
        oa = acca_s[...] / l_s[...]
        ob = accb_s[...]
        for t in range(dec_t):
            rows = slice(N_HEADS * t, N_HEADS * (t + 1))
            oa_ref[pl.ds(t * dec_b + b, 1), :] = jnp.sum(oa[rows] * hm, axis=0, keepdims=True)
            ob_ref[pl.ds(t * dec_b + b, 1), :] = jnp.sum(ob[rows] * hm, axis=0, keepdims=True)


def _sample_attention(layer, page_table, qa, qb, kan, van, kbn, vbn, lfn_t,
                      cache_a_kt, cache_a_vt, cache_a_lft, cache_b_kt, cache_b_vt, *, dec_b, dec_t):
    n_pg = PAGES_PER_STEP
    n_pages = page_table.shape[0] // dec_b
    nk = PAGES_PER_BLOCK * PAGE_SIZE
    nr = dec_t * N_HEADS
    m = dec_b * dec_t
    r = lax.broadcasted_iota(jnp.int32, (nk, nk), 0)
    c = lax.broadcasted_iota(jnp.int32, (nk, nk), 1)
    tri = (r > c).astype(BF16)
    hm = (lax.broadcasted_iota(jnp.int32, (N_HEADS, WIDTH), 1) // HEAD_DIM
          == lax.broadcasted_iota(jnp.int32, (N_HEADS, WIDTH), 0)).astype(F32)

    def page_spec(i, rows):
        def index(b, j, pt):
            pg = n_pages - jnp.maximum(j, 1) * n_pg + i
            return (layer, pt[b * n_pages + pg], 0, 0)
        return pl.BlockSpec((None, None, rows, PAGE_SIZE), index)

    full = lambda shape: pl.BlockSpec(shape, lambda b, j, pt: (0,) * len(shape))
    caches = (cache_a_kt, cache_a_vt, cache_b_kt, cache_b_vt, cache_a_lft)
    heights = (WIDTH, WIDTH, WIDTH, WIDTH, N_HEADS)
    page_specs = [page_spec(i, h) for h in heights for i in range(n_pg)]
    page_args = [cch for cch in caches for _ in range(n_pg)]
    grid_spec = pltpu.PrefetchScalarGridSpec(
        num_scalar_prefetch=1,
        grid=(dec_b, 1 + n_pages // n_pg),
        in_specs=[full((m, WIDTH))] * 6 + [full((N_HEADS, m)), full((nk, nk)), full((N_HEADS, WIDTH))]
        + page_specs,
        out_specs=[full((m, WIDTH)), full((m, WIDTH))],
        scratch_shapes=[pltpu.VMEM((nr, WIDTH), F32)] * 4 + [pltpu.VMEM((nr, 1), F32)] * 4,
    )
    return pl.pallas_call(
        functools.partial(_sample_attn_kernel, n_pg=n_pg, dec_b=dec_b, dec_t=dec_t),
        grid_spec=grid_spec,
        out_shape=[jax.ShapeDtypeStruct((m, WIDTH), F32)] * 2,
        compiler_params=_params(2),
        name="sample_attention",
    )(page_table, qa, qb, kan, van, kbn, vbn, lfn_t, tri, hm, *page_args)


def _merge_kernel(x_ref, oa_ref, ob_ref, gb_ref, u_ref, uprev_ref, st_ref, cw_ref, gpre_ref, gpost_ref,
                  wg_ref, bg_ref, wpa_ref, wpb_ref, wpc_ref, wo_ref, xo_ref, nb_ref, ext_s,
                  *, tm, shift, halo, tiles_per_seq):
    i = pl.program_id(0)
    x = x_ref[...]
    h = _rms(x, gpre_ref[...]).astype(BF16)
    u = u_ref[...]
    seq_start = i % tiles_per_seq == 0
    ext_s[0:halo, :] = jnp.where(seq_start, st_ref[0], uprev_ref[...])
    ext_s[halo:halo + tm, :] = u
    conv = (cw_ref[0:1, :] * ext_s[halo - 2 * shift:halo - 2 * shift + tm, :]
            + cw_ref[1:2, :] * ext_s[halo - shift:halo - shift + tm, :]
            + cw_ref[2:3, :] * u)
    yc = gb_ref[...].astype(F32) * conv
    mixed = jnp.zeros((tm, D_MODEL), F32)
    branches = ((oa_ref[...].astype(BF16), wpa_ref), (ob_ref[...].astype(BF16), wpb_ref), (yc.astype(BF16), wpc_ref))
    for br, (o, wp_ref) in enumerate(branches):
        cols = slice(D_MODEL * br, D_MODEL * (br + 1))
        gate = jax.nn.sigmoid(_nn(h, wg_ref[:, cols]) + bg_ref[:, cols])
        mixed = mixed + gate * _nn(o, wp_ref[...])
    y = _nn(mixed.astype(BF16), wo_ref[...])
    xo_ref[...] = x + _rms(y, gpost_ref[...])

    @pl.when(i % tiles_per_seq == tiles_per_seq - 1)
    def _():
        nb_ref[0] = ext_s[halo + tm - 2 * shift:halo + tm, :]


def _merge(x, oa, ob, gb, u, state_halo, conv_w, g_pre, g_post, w_gate, b_gate, wpa, wpb, wpc, wo,
           *, tm, shift, tiles_per_seq):
    m = x.shape[0]
    halo = state_halo.shape[1]
    n_groups = m // (tm * tiles_per_seq)
    row = lambda i: (i, 0)
    const = lambda i: (0, 0)
    wide = pl.BlockSpec((tm, WIDTH), row)
    return pl.pallas_call(
        functools.partial(_merge_kernel, tm=tm, shift=shift, halo=halo, tiles_per_seq=tiles_per_seq),
        grid=(m // tm,),
        in_specs=[pl.BlockSpec((tm, D_MODEL), row), wide, wide, wide, wide,
                  pl.BlockSpec((halo, WIDTH), lambda i: (jnp.maximum(i * (tm // halo) - 1, 0), 0)),
                  pl.BlockSpec((1, halo, WIDTH), lambda i: (i // tiles_per_seq, 0, 0)),
                  pl.BlockSpec((CONV_W, WIDTH), const),
                  pl.BlockSpec((1, D_MODEL), const), pl.BlockSpec((1, D_MODEL), const),
                  pl.BlockSpec(w_gate.shape, const), pl.BlockSpec((1, N_BRANCH * D_MODEL), const),
                  pl.BlockSpec(wpa.shape, const), pl.BlockSpec(wpb.shape, const),
                  pl.BlockSpec(wpc.shape, const), pl.BlockSpec(wo.shape, const)],
        out_specs=[pl.BlockSpec((tm, D_MODEL), row),
                   pl.BlockSpec((1, 2 * shift, WIDTH), lambda i: (i // tiles_per_seq, 0, 0))],
        out_shape=[jax.ShapeDtypeStruct((m, D_MODEL), F32),
                   jax.ShapeDtypeStruct((n_groups, 2 * shift, WIDTH), F32)],
        scratch_shapes=[pltpu.VMEM((halo + tm, WIDTH), F32)],
        compiler_params=_params(1),
        name="merge",
    )(x, oa, ob, gb, u, u, state_halo, conv_w, g_pre, g_post, w_gate, b_gate, wpa, wpb, wpc, wo)


def _ffn_kernel(x_ref, gpre_ref, gpost_ref, wg_ref, wu_ref, wd_ref, o_ref, h_s, acc_s):
    j = pl.program_id(1)

    @pl.when(j == 0)
    def _():
        h_s[...] = _rms(x_ref[...], gpre_ref[...]).astype(BF16)
        acc_s[...] = jnp.zeros_like(acc_s)

    h = h_s[...]
    g = _nn(h, wg_ref[...])
    up = _nn(h, wu_ref[...])
    act = (g * jax.nn.sigmoid(g) * up).astype(BF16)
    acc_s[...] += _nn(act, wd_ref[...])

    @pl.when(j == pl.num_programs(1) - 1)
    def _():
        o_ref[...] = x_ref[...] + _rms(acc_s[...], gpost_ref[...])


def _ffn(x, g_pre, g_post, wg, wu, wd, *, tm, tf):
    m = x.shape[0]
    row = lambda i, j: (i, 0)
    const = lambda i, j: (0, 0)
    return pl.pallas_call(
        _ffn_kernel,
        grid=(m // tm, D_FF // tf),
        in_specs=[pl.BlockSpec((tm, D_MODEL), row),
                  pl.BlockSpec((1, D_MODEL), const), pl.BlockSpec((1, D_MODEL), const),
                  pl.BlockSpec((D_MODEL, tf), lambda i, j: (0, j)),
                  pl.BlockSpec((D_MODEL, tf), lambda i, j: (0, j)),
                  pl.BlockSpec((tf, D_MODEL), lambda i, j: (j, 0))],
        out_specs=pl.BlockSpec((tm, D_MODEL), row),
        out_shape=jax.ShapeDtypeStruct((m, D_MODEL), F32),
        scratch_shapes=[pltpu.VMEM((tm, D_MODEL), BF16), pltpu.VMEM((tm, D_MODEL), F32)],
        compiler_params=_params(2),
        name="ffn",
    )(x, g_pre, g_post, wg, wu, wd)


def kernel(x_prompt, x_sample, cache_a_k, cache_a_v, cache_a_logf, cache_b_k, cache_b_v, state_conv, page_table, w_in, b_f, conv_w, w_proj_a, w_proj_b, w_proj_c, w_gate, b_gate, w_o, g_mix_pre, g_mix_post, g_ffn_pre, g_ffn_post, w_ffn_gate, w_ffn_up, w_ffn_down):
    n_seq, t_len, _ = x_prompt.shape
    dec_b, dec_t, _ = x_sample.shape
    depth = w_in.shape[0]
    n_pool = cache_a_k.shape[1]
    mp = n_seq * t_len
    ms = dec_b * dec_t
    tq = ATT_BLOCK
    tm_p = 512

    xp = x_prompt.reshape(mp, D_MODEL)
    xs = jnp.transpose(x_sample, (1, 0, 2)).reshape(ms, D_MODEL)
    pt = page_table.reshape(-1)
    feature_major = lambda c: jnp.transpose(c, (0, 1, 3, 4, 2)).reshape(depth, n_pool, WIDTH, PAGE_SIZE)
    caches = (feature_major(cache_a_k), feature_major(cache_a_v), jnp.transpose(cache_a_logf, (0, 1, 3, 2)),
              feature_major(cache_b_k), feature_major(cache_b_v))
    r = lax.broadcasted_iota(jnp.int32, (tq + 2 * SUBLANES, tq), 0)
    c = lax.broadcasted_iota(jnp.int32, (tq + 2 * SUBLANES, tq), 1)
    tri = ((c > r) | (r >= tq)).astype(BF16)
    prompt_halo = jnp.zeros((n_seq, SUBLANES, WIDTH), F32)

    rows_p, rows_s = [], []
    new_kv = ()
    qkv = 3 * WIDTH
    for l in range(depth):
        wl = w_in[l]
        wt = jnp.concatenate([wl[:, 0:qkv], wl[:, qkv + N_HEADS:2 * qkv + N_HEADS]], axis=1).T.astype(BF16)
        wc = wl[:, 2 * qkv + N_HEADS:].astype(BF16)
        w_f = wl[:, qkv:qkv + N_HEADS]
        wf = jnp.pad(w_f, ((0, 0), (0, LANES - N_HEADS))).astype(BF16)
        wft = jnp.pad(w_f.T, ((0, 2 * SUBLANES - N_HEADS), (0, 0))).astype(BF16)
        g_pre, g_post = g_mix_pre[l].reshape(1, -1), g_mix_post[l].reshape(1, -1)
        f_pre, f_post = g_ffn_pre[l].reshape(1, -1), g_ffn_post[l].reshape(1, -1)
        merge_w = (conv_w[l], g_pre, g_post, w_gate[l].astype(BF16), b_gate[l].reshape(1, -1),
                   w_proj_a[l].astype(BF16), w_proj_b[l].astype(BF16), w_proj_c[l].astype(BF16),
                   w_o[l].astype(BF16))
        ffn_w = (f_pre, f_post, w_ffn_gate[l].astype(BF16), w_ffn_up[l].astype(BF16),
                 w_ffn_down[l].astype(BF16))

        (qat, qbt, ka16, kb16, *new_kv, vat16, vbt16, gb, u, logf_t) = _in_proj_prompt(
            l, depth, new_kv, xp, g_pre, wt, wc, wft, b_f[l], n_seq=n_seq, t_len=t_len, tm=tm_p, tq=tq)
        cum, cum_t = _cumsum(logf_t, tq=tq)
        nq = t_len // tq
        oa = _prompt_attention(
            _fox_kernel, "fox_attention", qat, ka16, vat16, (cum, cum_t),
            [pl.BlockSpec((t_len, N_HEADS), lambda b, p, i: (b, 0)),
             pl.BlockSpec((None, None, N_HEADS, tq), lambda b, p, i: (b, i, 0, 0))],
            [pltpu.VMEM((2, t_len, LANES), F32), pltpu.VMEM((2, nq, tq, tq), F32)], tq=tq)
        ob = _prompt_attention(
            _sb_kernel, "sb_attention", qbt, kb16, vbt16, (tri,),
            [pl.BlockSpec(tri.shape, lambda b, p, i: (0, 0))],
            [pltpu.VMEM((2, nq, tq, tq), BF16), pltpu.VMEM((2, nq, tq, tq), F32)], tq=tq)
        xp, nb_p = _merge(xp, oa, ob, gb, u, prompt_halo, *merge_w,
                          tm=tm_p, shift=1, tiles_per_seq=t_len // tm_p)
        xp = _ffn(xp, *ffn_w, tm=1024, tf=256)
        rows_p.append((logf_t, nb_p))

        (qa, ka, va, qb, kb, vb, gb, u, logf, logf_t) = _in_proj_sample(
            xs, g_pre, wt, wc, wf, wft, b_f[l])
        oa, ob = _sample_attention(l, pt, qa, qb, ka, va, kb, vb, logf_t, *caches,
                                   dec_b=dec_b, dec_t=dec_t)
        state = jnp.transpose(state_conv[l], (1, 0, 2)).reshape(1, (CONV_W - 1) * dec_b, WIDTH)
        xs, nb_s = _merge(xs, oa, ob, gb, u, state, *merge_w, tm=ms, shift=dec_b, tiles_per_seq=1)
        xs = _ffn(xs, *ffn_w, tm=ms, tf=256)
        rows_s.append((ka, va, logf, kb, vb, nb_s))

    stack = lambda rows, i: jnp.stack([r[i] for r in rows], axis=0)

    def new_kv_p(i):
        a = new_kv[i].reshape(depth, n_seq, N_HEADS, HEAD_DIM, t_len)
        return jnp.transpose(a, (0, 1, 4, 2, 3))

    def stack_s(i, tail):
        a = stack(rows_s, i).reshape(depth, -1, dec_b, *tail)
        return jnp.swapaxes(a, 1, 2)

    heads = (N_HEADS, HEAD_DIM)
    return (xp.reshape(n_seq, t_len, D_MODEL),
            jnp.transpose(xs.reshape(dec_t, dec_b, D_MODEL), (1, 0, 2)),
            new_kv_p(0), new_kv_p(1), jnp.transpose(stack(rows_p, 0), (0, 1, 3, 2)),
            new_kv_p(2), new_kv_p(3), stack(rows_p, 1),
            stack_s(0, heads), stack_s(1, heads), stack_s(2, (N_HEADS,)),
            stack_s(3, heads), stack_s(4, heads), stack_s(5, (WIDTH,)))
```

```python
import functools

import jax
import jax.numpy as jnp
from jax import lax
from jax.experimental import pallas as pl
from jax.experimental.pallas import tpu as pltpu

D_MODEL = 1024
HEAD_DIM = 64
N_HEADS = 8
WIDTH = N_HEADS * HEAD_DIM
CONV_W = 3
D_FF = 2816
N_BRANCH = 3
PAGE_SIZE = 128
RMS_EPS = 1e-6
SCALE = HEAD_DIM ** -0.5
LOG2E = 1.4426950408889634

LANES = 128
SUBLANES = 8
VMEM_LIMIT_BYTES = 52 * 1024 * 1024

BF16 = jnp.bfloat16
F32 = jnp.float32
NEG = -1e30
SIGN_BIT = -2 ** 31

ATT_BLOCK = 256
PAGES_PER_STEP = 8
PAGES_PER_BLOCK = 4


def _nn(a, b):
    return jnp.dot(a, b, preferred_element_type=F32)


def _nt(a, b):
    return lax.dot_general(a, b, (((1,), (1,)), ((), ())), preferred_element_type=F32)


def _rms(x, g):
    ms = jnp.mean(x * x, axis=-1, keepdims=True)
    return x * lax.rsqrt(ms + RMS_EPS) * g


def _log_sigmoid(x):
    return jnp.minimum(x, 0.0) - jnp.log1p(jnp.exp(-jnp.abs(x)))


def _log_sigmoid_pair(z):
    soft = jnp.log1p(jnp.exp(-jnp.abs(z)))
    return jnp.minimum(-z, 0.0) - soft, jnp.minimum(z, 0.0) - soft


def _split_bf16(x, n):
    parts = []
    r = x
    for i in range(n):
        p = r.astype(BF16)
        parts.append(p)
        if i + 1 < n:
            r = r - p.astype(F32)
    return parts


def _params(n_axes):
    return pltpu.CompilerParams(dimension_semantics=("arbitrary",) * n_axes,
                                vmem_limit_bytes=VMEM_LIMIT_BYTES)


def _in_proj_prompt_kernel(x_ref, g_ref, wt_ref, wc_ref, wft_ref, bft_ref, *refs, tq, n_carried):
    (qat_ref, qbt_ref, ka_ref, kb_ref, kat_ref, vat_ref, kbt_ref, vbt_ref,
     vat16_ref, vbt16_ref, gb_ref, u_ref, lft_ref) = refs[n_carried:]
    h = _rms(x_ref[...], g_ref[...]).astype(BF16)
    tm = h.shape[0]
    rows = lambda k: wt_ref[WIDTH * k:WIDTH * (k + 1), :]
    for q_ref, kt_ref, k_ref, vt_ref, vt16_ref, base in ((qat_ref, kat_ref, ka_ref, vat_ref, vat16_ref, 0),
                                                         (qbt_ref, kbt_ref, kb_ref, vbt_ref, vbt16_ref, 3)):
        qt = (_nt(rows(base), h) * (SCALE * LOG2E)).astype(BF16)
        kt_ref[...] = _nt(rows(base + 1), h)
        k_ref[...] = _nt(h, rows(base + 1)).astype(BF16)
        vt = _nt(rows(base + 2), h)
        vt_ref[...] = vt
        for c in range(tm // tq):
            q_ref[c] = qt[:, tq * c:tq * (c + 1)]
            vt16_ref[c] = vt[:, tq * c:tq * (c + 1)].astype(BF16)
    gb_ref[...] = _nn(h, wc_ref[:, 0:WIDTH]).astype(BF16)
    u_ref[...] = _nn(h, wc_ref[:, WIDTH:2 * WIDTH]) * _nn(h, wc_ref[:, 2 * WIDTH:3 * WIDTH])
    ft = _nt(wft_ref[...], h)[:N_HEADS, :]
    lft_ref[...] = _log_sigmoid(ft + bft_ref[...])


N_NEW_KV = 4
FIRST_NEW_KV_OUT = 4


def _in_proj_prompt(layer, depth, carried, x, g, wt, wc, wft, b_f, *, n_seq, t_len, tm, tq):
    m = x.shape[0]
    tps = t_len // tm
    nq = t_len // tq
    row = lambda i: (i, 0)
    const = lambda i: (0, 0)
    tok = lambda dt: (jax.ShapeDtypeStruct((m, WIDTH), dt), pl.BlockSpec((tm, WIDTH), row))
    new_t = (jax.ShapeDtypeStruct((depth, n_seq, WIDTH, t_len), F32),
             pl.BlockSpec((None, None, WIDTH, tm), lambda i: (layer, i // tps, 0, i % tps)))
    blocked = (jax.ShapeDtypeStruct((n_seq, nq, WIDTH, tq), BF16),
               pl.BlockSpec((None, tm // tq, WIDTH, tq), lambda i: (i // tps, i % tps, 0, 0)))
    outs = [blocked, blocked, tok(BF16), tok(BF16)] + [new_t] * N_NEW_KV + [blocked, blocked, tok(BF16), tok(F32),
            (jax.ShapeDtypeStruct((n_seq, N_HEADS, t_len), F32),
             pl.BlockSpec((None, N_HEADS, tm), lambda i: (i // tps, 0, i % tps)))]
    in_specs = [pl.BlockSpec((tm, D_MODEL), row),
                pl.BlockSpec((1, D_MODEL), const),
                pl.BlockSpec(wt.shape, const),
                pl.BlockSpec(wc.shape, const),
                pl.BlockSpec(wft.shape, const),
                pl.BlockSpec((N_HEADS, 1), const)]
    assert len(carried) in (0, N_NEW_KV)
    aliases = {len(in_specs) + k: FIRST_NEW_KV_OUT + k for k in range(len(carried))}
    return pl.pallas_call(
        functools.partial(_in_proj_prompt_kernel, tq=tq, n_carried=len(carried)),
        grid=(m // tm,),
        in_specs=in_specs + [pl.BlockSpec(memory_space=pl.ANY)] * len(carried),
        out_specs=[o[1] for o in outs],
        out_shape=[o[0] for o in outs],
        input_output_aliases=aliases,
        compiler_params=_params(1),
        name="in_proj_prompt",
    )(x, g, wt, wc, wft, b_f.reshape(N_HEADS, 1), *carried)


def _in_proj_sample_kernel(x_ref, g_ref, wt_ref, wc_ref, wf_ref, wft_ref, bf_ref, bft_ref,
                           qa_ref, ka_ref, va_ref, qb_ref, kb_ref, vb_ref, gb_ref, u_ref, lf_ref, lft_ref):
    h = _rms(x_ref[...], g_ref[...]).astype(BF16)
    for k, o_ref in enumerate((qa_ref, ka_ref, va_ref, qb_ref, kb_ref, vb_ref)):
        y = _nt(h, wt_ref[WIDTH * k:WIDTH * (k + 1), :])
        o_ref[...] = y * SCALE if k % 3 == 0 else y
    gb_ref[...] = _nn(h, wc_ref[:, 0:WIDTH]).astype(BF16)
    u_ref[...] = _nn(h, wc_ref[:, WIDTH:2 * WIDTH]) * _nn(h, wc_ref[:, 2 * WIDTH:3 * WIDTH])
    f = _nn(h, wf_ref[...])[:, :N_HEADS]
    lf_ref[...] = _log_sigmoid(f + bf_ref[...])
    ft = _nt(wft_ref[...], h)[:N_HEADS, :]
    lft_ref[...] = _log_sigmoid(ft + bft_ref[...])


def _in_proj_sample(x, g, wt, wc, wf, wft, b_f):
    m = x.shape[0]
    full = lambda shape: pl.BlockSpec(shape, lambda i: (0,) * len(shape))
    tok = lambda dt: jax.ShapeDtypeStruct((m, WIDTH), dt)
    out_shape = [tok(F32)] * 6 + [tok(BF16), tok(F32),
                                  jax.ShapeDtypeStruct((m, N_HEADS), F32), jax.ShapeDtypeStruct((N_HEADS, m), F32)]
    args = (x, g, wt, wc, wf, wft, b_f.reshape(1, N_HEADS), b_f.reshape(N_HEADS, 1))
    return pl.pallas_call(
        _in_proj_sample_kernel,
        grid=(1,),
        in_specs=[full(a.shape) for a in args],
        out_specs=[full(s.shape) for s in out_shape],
        out_shape=out_shape,
        compiler_params=_params(1),
        name="in_proj_sample",
    )(*args)


def _cumsum_kernel(lt_ref, c_ref, ct_ref, *, t_len, tq):
    nb = t_len // LANES
    x = jnp.concatenate([lt_ref[:, LANES * k:LANES * (k + 1)] for k in range(nb)], axis=0)
    n = nb * N_HEADS
    r = lax.broadcasted_iota(jnp.int32, (LANES, LANES), 0)
    c = lax.broadcasted_iota(jnp.int32, (LANES, LANES), 1)
    incl = (r <= c).astype(BF16)
    y = sum(_nn(p, incl) for p in _split_bf16(x, 3))
    r = lax.broadcasted_iota(jnp.int32, (n, n), 0)
    c = lax.broadcasted_iota(jnp.int32, (n, n), 1)
    earlier = ((c // N_HEADS < r // N_HEADS) & (c % N_HEADS == r % N_HEADS)).astype(BF16)
    offs = sum(_nn(earlier, p) for p in _split_bf16(y, 3))[:, LANES - 1:LANES]
    z = (y + offs) * LOG2E
    zt = z.T
    per_q = tq // LANES
    for k in range(nb):
        ct_ref[k // per_q, :, LANES * (k % per_q):LANES * (k % per_q + 1)] = z[N_HEADS * k:N_HEADS * (k + 1), :]
        c_ref[LANES * k:LANES * (k + 1), :] = zt[:, N_HEADS * k:N_HEADS * (k + 1)]


def _cumsum(logf_t, *, tq):
    n_seq, _, t_len = logf_t.shape
    nq = t_len // tq
    return pl.pallas_call(
        functools.partial(_cumsum_kernel, t_len=t_len, tq=tq),
        grid=(n_seq,),
        in_specs=[pl.BlockSpec((None, N_HEADS, t_len), lambda b: (b, 0, 0))],
        out_specs=[pl.BlockSpec((t_len, N_HEADS), lambda b: (b, 0)),
                   pl.BlockSpec((None, nq, N_HEADS, tq), lambda b: (b, 0, 0, 0))],
        out_shape=[jax.ShapeDtypeStruct((n_seq * t_len, N_HEADS), F32),
                   jax.ShapeDtypeStruct((n_seq, nq, N_HEADS, tq), F32)],
        compiler_params=_params(1),
        name="logf_cumsum",
    )(logf_t)


def _head_pair_qt(qt):
    low =lax.broadcasted_iota(jnp.int32, qt.shape, 0) < HEAD_DIM
    zero = jnp.zeros_like(qt)
    return jnp.where(low, qt, zero), jnp.where(low, zero, qt)


def _head_column(c, head):
    hcol = lax.broadcasted_iota(jnp.int32, c.shape, 1)
    return jnp.sum(jnp.where(hcol == head, c, 0.0), axis=1, keepdims=True)


def _head_row(ct, head):
    hrow = lax.broadcasted_iota(jnp.int32, ct.shape, 0)
    return jnp.sum(jnp.where(hrow == head, ct, 0.0), axis=0, keepdims=True)


def _fox_kernel(qt_ref, k_ref, vt_ref, c_ref, ct_ref, o_ref, ck_s, s_s, *, tq):
    pair = pl.program_id(1)
    c_all = c_ref[...]
    for e in (0, 1):
        ck_s[e] = jnp.broadcast_to(_head_column(c_all, 2 * pair + e), ck_s.shape[1:])

    def query_block(qi, carry):
        o_ref[pl.ds(pl.multiple_of(qi * tq, tq), tq), :] = _fox_query_block(
            qi, pair, qt_ref[qi], ct_ref[qi], k_ref, vt_ref, ck_s, s_s, tq).astype(o_ref.dtype)
        return carry

    lax.fori_loop(0, qt_ref.shape[0], query_block, 0)


def _fox_query_block(qi, pair, qt, cq_all, k_ref, vt_ref, ck_s, s_s, tq):
    qts = _head_pair_qt(qt)
    cq = [_head_row(cq_all, 2 * pair + e) for e in (0, 1)]
    key = lax.broadcasted_iota(jnp.int32, (tq, tq), 0)
    qry = lax.broadcasted_iota(jnp.int32, (tq, tq), 1)
    reps = tq // LANES
    n_full = qi // 2

    def logits(js, m, masked):
        starts = [pl.multiple_of(j * tq, tq) for j in js]
        qk = [[_nn(k_ref[pl.ds(st, tq), :], qts[e]) for e in (0, 1)] for st in starts]
        m = list(m)
        for b, (j, st) in enumerate(zip(js, starts)):
            for e in (0, 1):
                ck = ck_s[e, pl.ds(st, tq), :]
                s = qk[b][e] + (cq[e] - jnp.concatenate([ck] * reps, axis=1))
                if masked:
                    s = jnp.where(key + (j - qi) * tq <= qry, s, NEG)
                s_s[e, j] = s
                m[e] = jnp.maximum(m[e], jnp.max(s, axis=0, keepdims=True))
        return tuple(m)

    even = qi % 2 == 0
    m = lax.fori_loop(0, n_full, lambda i, m: logits((2 * i, 2 * i + 1), m, False),
                      (jnp.full((1, tq), NEG, F32),) * 2)
    m = lax.cond(even, lambda m: logits((qi,), m, True), lambda m: logits((qi - 1, qi), m, True), m)

    ones = jnp.ones((2 * SUBLANES, tq), BF16)

    def weigh(js, acc):
        p = [[jnp.exp2(s_s[e, j] - m[e]).astype(BF16) for e in (0, 1)] for j in js]
        pv = [[_nn(jnp.concatenate([vt_ref[j, HEAD_DIM * e:HEAD_DIM * (e + 1), :], ones], axis=0), p[b][e])
               for e in (0, 1)] for b, j in enumerate(js)]
        return tuple(acc[e] + sum(x[e] for x in pv) for e in (0, 1))

    acc = lax.fori_loop(0, n_full, lambda i, acc: weigh((2 * i, 2 * i + 1), acc),
                        (jnp.zeros((HEAD_DIM + 2 * SUBLANES, tq), F32),) * 2)
    acc = lax.cond(even, lambda acc: weigh((qi,), acc), lambda acc: weigh((qi - 1, qi), acc), acc)
    out = [acc[e][0:HEAD_DIM] / acc[e][HEAD_DIM:HEAD_DIM + 1] for e in (0, 1)]
    return jnp.concatenate(out, axis=0).T


def _sb_kernel(qt_ref, k_ref, vt_ref, tri_ref, o_ref, lr_s, pos_s, *, tq):
    def query_block(qi, carry):
        o_ref[pl.ds(pl.multiple_of(qi * tq, tq), tq), :] = _sb_query_block(
            qi, qt_ref[qi], k_ref, vt_ref, tri_ref, lr_s, pos_s, tq).astype(o_ref.dtype)
        return carry

    lax.fori_loop(0, qt_ref.shape[0], query_block, 0)


def _sb_query_block(qi, qt, k_ref, vt_ref, tri_ref, lr_s, pos_s, tq):
    nqts = [-x for x in _head_pair_qt(qt)]
    key = lax.broadcasted_iota(jnp.int32, (tq, tq), 0)
    qry = lax.broadcasted_iota(jnp.int32, (tq, tq), 1)
    n_full = qi // 2

    def gates(js, masked):
        nz = [[_nn(k_ref[pl.ds(pl.multiple_of(j * tq, tq), tq), :], nqts[e]) for e in (0, 1)] for j in js]
        for b, j in enumerate(js):
            for e in (0, 1):
                n = nz[b][e]
                neg_abs = lax.bitcast_convert_type(lax.bitcast_convert_type(n, jnp.int32) | SIGN_BIT, F32)
                lr = jnp.minimum(n, 0.0) - jnp.log2(1.0 + jnp.exp2(neg_abs))
                ls_pos = lr - n
                if masked:
                    strict = key + (j - qi) * tq < qry
                    lr = jnp.where(strict, lr, 0.0)
                    ls_pos = jnp.where(strict, ls_pos, NEG)
                lr_s[e, j] = lr.astype(BF16)
                pos_s[e, j] = ls_pos

    def unmasked(i, carry):
        gates((2 * i, 2 * i + 1), False)
        return carry

    even = qi % 2 == 0
    lax.fori_loop(0, n_full, unmasked, 0)

    @pl.when(even)
    def _():
        gates((qi,), True)

    @pl.when(jnp.logical_not(even))
    def _():
        gates((qi - 1, qi), True)

    tri = tri_ref[...]

    def weigh(js, carry):
        sums = [[_nn(tri, lr_s[e, j]) for e in (0, 1)] for j in js]
        rest =[[carry[e][0]] for e in (0, 1)]
        for b in range(len(js)):
            for e in (0, 1):
                rest[e].append(rest[e][b] + sums[b][e][tq:tq + 1])
        a = [[jnp.exp2(pos_s[e, j] + (sums[b][e][0:tq] + rest[e][b])).astype(BF16) for e in (0, 1)]
             for b, j in enumerate(js)]
        pv = [[_nn(vt_ref[j, HEAD_DIM * e:HEAD_DIM * (e + 1), :], a[b][e]) for e in (0, 1)]
              for b, j in enumerate(js)]
        return tuple((rest[e][-1], carry[e][1] + sum(x[e] for x in pv)) for e in (0, 1))

    init = ((jnp.zeros((1, tq), F32), jnp.zeros((HEAD_DIM, tq), F32)),) * 2
    carry = lax.cond(even, lambda c: weigh((qi,), c), lambda c: weigh((qi, qi - 1), c), init)
    top = 2 * n_full - 1
    (_, a0), (_, a1) = lax.fori_loop(0, n_full, lambda i, c: weigh((top - 2 * i, top - 2 * i - 1), c), carry)
    return jnp.concatenate([a0, a1], axis=0).T


def _prompt_attention(kernel_fn, name, qt16, k, vt16, extra, extra_specs, scratch, *, tq):
    n_seq, nq = vt16.shape[:2]
    t_len = nq * tq
    n_pair = WIDTH // LANES
    blocked = pl.BlockSpec((None, nq, LANES, tq), lambda b, p: (b, 0, p, 0))
    tokens = pl.BlockSpec((t_len, LANES), lambda b, p: (b, p))
    return pl.pallas_call(
        functools.partial(kernel_fn, tq=tq),
        grid=(n_seq, n_pair),
        in_specs=[blocked, tokens, blocked] + extra_specs,
        out_specs=tokens,
        out_shape=jax.ShapeDtypeStruct(k.shape, BF16),
        scratch_shapes=scratch,
        compiler_params=_params(2),
        name=name,
    )(qt16, k, vt16, *extra)


def _sample_attn_kernel(pt_ref, qa_ref, qb_ref, kan_ref, van_ref, kbn_ref, vbn_ref, lfn_ref,
                        tri_ref, hm_ref, *rest, n_pg, dec_b, dec_t):
    del pt_ref
    pg_ak, pg_av, pg_bk, pg_bv, pg_lf = (rest[n_pg * i:n_pg * (i + 1)] for i in range(5))
    oa_ref, ob_ref = rest[5 * n_pg:5 * n_pg + 2]
    qa_s, qb_s, acca_s, accb_s, m_s, l_s, dec_s, rest_s = rest[5 * n_pg + 2:]
    b = pl.program_id(0)
    j = pl.program_id(1)
    nr = dec_t * N_HEADS
    hm = hm_ref[...]

    def token_rows(ref):
        return [ref[pl.ds(t * dec_b + b, 1), :] for t in range(dec_t)]

    def bf16_exact(x):
        return x.astype(BF16).astype(F32)

    def lanes(xs):
        return xs[0] if len(xs) == 1 else jnp.concatenate(xs, axis=1)

    def update(blocks, own):
        n = PAGE_SIZE * len(blocks[0][0])
        rowsum = lambda x: jnp.sum(x, axis=1, keepdims=True)
        sa = [_nn(qa_s[...], lanes(blk[0])) for blk in blocks]
        sb = [_nn(qb_s[...], lanes(blk[2])) for blk in blocks]
        lf = [jnp.concatenate([lanes(blk[4])] * dec_t, axis=0) for blk in blocks]
        ls = [_log_sigmoid_pair(x) for x in sb]
        if own:
            key = lax.broadcasted_iota(jnp.int32, (nr, n), 1)
            t_row = lax.broadcasted_iota(jnp.int32, (nr, n), 0) // N_HEADS
            valid_a = key <= t_row
            valid_b = key < t_row
            lr = [jnp.where(valid_b, x[0], 0.0) for x in ls]
        else:
            lr = [x[0] for x in ls]
        tri = tri_ref[0:n, 0:n]
        later = [_nn(jnp.concatenate(_split_bf16(f, 2) + _split_bf16(r, 2), axis=0), tri)
                 for f, r in zip(lf, lr)]
        later_a = [x[0:nr] + x[nr:2 * nr] for x in later]
        later_b = [x[2 * nr:3 * nr] + x[3 * nr:4 * nr] for x in later]

        dec = [-rowsum(jnp.where(key == t_row, later_a[0], 0.0)) if own else dec_s[...]]
        rest = [rest_s[...]]
        for f, r in zip(lf, lr):
            dec.append(dec[-1] + rowsum(f))
            rest.append(rest[-1] + rowsum(r))
        s = [x + (la + d) for x, la, d in zip(sa, later_a, dec)]
        if own:
            s = [jnp.where(valid_a, x, NEG) for x in s]
        m_old = m_s[...]
        m_new = functools.reduce(jnp.maximum, [m_old] + [jnp.max(x, axis=1, keepdims=True) for x in s])
        alpha = jnp.exp(m_old - m_new)
        p = [jnp.exp(x - m_new) for x in s]
        a = [jnp.exp(x[1] + lb + r) for x, lb, r in zip(ls, later_b, rest)]
        if own:
            a = [jnp.where(valid_b, x, 0.0) for x in a]
        pv_a = sum(_nt(bf16_exact(x), lanes(blk[1])) for x, blk in zip(p, blocks))
        pv_b = sum(_nt(bf16_exact(x), lanes(blk[3])) for x, blk in zip(a, blocks))
        l_s[...] = alpha * l_s[...] + sum(rowsum(x) for x in p)
        acca_s[...] = alpha * acca_s[...] + pv_a
        m_s[...] = m_new
        dec_s[...] = dec[-1]
        accb_s[...] = accb_s[...] + pv_b
        rest_s[...] = rest[-1]

    @pl.when(j == 0)
    def _():
        def block_diag_q(ref):
            return bf16_exact(jnp.concatenate(
                [jnp.broadcast_to(r, (N_HEADS, WIDTH)) * hm for r in token_rows(ref)], axis=0))

        def own_tile(ref):
            sub = lax.broadcasted_iota(jnp.int32, (SUBLANES, WIDTH), 0)
            top = jnp.zeros((SUBLANES, WIDTH), F32)
            for t, r in enumerate(token_rows(ref)):
                top = jnp.where(sub == t, jnp.broadcast_to(r, (SUBLANES, WIDTH)), top)
            return jnp.concatenate([top, jnp.zeros((PAGE_SIZE - SUBLANES, WIDTH), F32)], axis=0).T

        qa_s[...] = block_diag_q(qa_ref)
        qb_s[...] = block_diag_q(qb_ref)
        lane = lax.broadcasted_iota(jnp.int32, (N_HEADS, dec_b * dec_t), 1)
        pos = lax.broadcasted_iota(jnp.int32, (N_HEADS, PAGE_SIZE), 1)
        lfn = lfn_ref[...]
        lf0 = jnp.zeros((N_HEADS, PAGE_SIZE), F32)
        for t in range(dec_t):
            col = jnp.sum(jnp.where(lane == t * dec_b + b, lfn, 0.0), axis=1, keepdims=True)
            lf0 = jnp.where(pos == t, col, lf0)
        acca_s[...] = jnp.zeros((nr, WIDTH), F32)
        accb_s[...] = jnp.zeros((nr, WIDTH), F32)
        m_s[...] = jnp.full((nr, 1), NEG, F32)
        l_s[...] = jnp.zeros((nr, 1), F32)
        dec_s[...] = jnp.zeros((nr, 1), F32)
        rest_s[...] = jnp.zeros((nr, 1), F32)
        update([([own_tile(kan_ref)], [own_tile(van_ref)], [own_tile(kbn_ref)], [own_tile(vbn_ref)], [lf0])], True)

    @pl.when(j > 0)
    def _():
        update([[[pg[...] for pg in pgs[hi - PAGES_PER_BLOCK:hi]] for pgs in (pg_ak, pg_av, pg_bk, pg_bv, pg_lf)]
                for hi in range(n_pg, 0, -PAGES_PER_BLOCK)], False)

    @pl.when(j == pl.num_programs(1) - 1)
    def _():
        oa = acca_s[...] / l_s[...]
        ob = accb_s[...]
        for t in range(dec_t):
            rows = slice(N_HEADS * t, N_HEADS * (t + 1))
            oa_ref[pl.ds(t * dec_b + b, 1), :] = jnp.sum(oa[rows] * hm, axis=0, keepdims=True)
            ob_ref[pl.ds(t * dec_b + b, 1), :] = jnp.sum(ob[rows] * hm, axis=0, keepdims=True)


def _sample_attention(layer, page_table, qa, qb, kan, van, kbn, vbn, lfn_t,
                      cache_a_kt, cache_a_vt, cache_a_lft, cache_b_kt, cache_b_vt, *, dec_b, dec_t):
    n_pg = PAGES_PER_STEP
    n_pages = page_table.shape[0] // dec_b
    nk = PAGES_PER_BLOCK * PAGE_SIZE
    nr = dec_t * N_HEADS
    m = dec_b * dec_t
    r = lax.broadcasted_iota(jnp.int32, (nk, nk), 0)
    c = lax.broadcasted_iota(jnp.int32, (nk, nk), 1)
    tri = (r > c).astype(BF16)
    hm = (lax.broadcasted_iota(jnp.int32, (N_HEADS, WIDTH), 1) // HEAD_DIM
          == lax.broadcasted_iota(jnp.int32, (N_HEADS, WIDTH), 0)).astype(F32)

    def page_spec(i, rows):
        def index(b, j, pt):
            pg = n_pages - jnp.maximum(j, 1) * n_pg + i
            return (layer, pt[b * n_pages + pg], 0, 0)
        return pl.BlockSpec((None, None, rows, PAGE_SIZE), index)

    full = lambda shape: pl.BlockSpec(shape, lambda b, j, pt: (0,) * len(shape))
    caches = (cache_a_kt, cache_a_vt, cache_b_kt, cache_b_vt, cache_a_lft)
    heights = (WIDTH, WIDTH, WIDTH, WIDTH, N_HEADS)
    page_specs = [page_spec(i, h) for h in heights for i in range(n_pg)]
    page_args = [cch for cch in caches for _ in range(n_pg)]
    grid_spec = pltpu.PrefetchScalarGridSpec(
        num_scalar_prefetch=1,
        grid=(dec_b, 1 + n_pages // n_pg),
        in_specs=[full((m, WIDTH))] * 6 + [full((N_HEADS, m)), full((nk, nk)), full((N_HEADS, WIDTH))]
        + page_specs,
        out_specs=[full((m, WIDTH)), full((m, WIDTH))],
        scratch_shapes=[pltpu.VMEM((nr, WIDTH), F32)] * 4 + [pltpu.VMEM((nr, 1), F32)] * 4,
    )
    return pl.pallas_call(
        functools.partial(_sample_attn_kernel, n_pg=n_pg, dec_b=dec_b, dec_t=dec_t),
        grid_spec=grid_spec,
        out_shape=[jax.ShapeDtypeStruct((m, WIDTH), F32)] * 2,
        compiler_params=_params(2),
        name="sample_attention",
    )(page_table, qa, qb, kan, van, kbn, vbn, lfn_t, tri, hm, *page_args)


def _merge_kernel(x_ref, oa_ref, ob_ref, gb_ref, u_ref, uprev_ref, st_ref, cw_ref, gpre_ref, gpost_ref,
                  wg_ref, bg_ref, wpa_ref, wpb_ref, wpc_ref, wo_ref, xo_ref, nb_ref, ext_s,
                  *, tm, shift, halo, tiles_per_seq):
    i = pl.program_id(0)
    x = x_ref[...]
    h = _rms(x, gpre_ref[...]).astype(BF16)
    u = u_ref[...]
    seq_start = i % tiles_per_seq == 0
    ext_s[0:halo, :] = jnp.where(seq_start, st_ref[0], uprev_ref[...])
    ext_s[halo:halo + tm, :] = u
    conv = (cw_ref[0:1, :] * ext_s[halo - 2 * shift:halo - 2 * shift + tm, :]
            + cw_ref[1:2, :] * ext_s[halo - shift:halo - shift + tm, :]
            + cw_ref[2:3, :] * u)
    yc = gb_ref[...].astype(F32) * conv
    mixed = jnp.zeros((tm, D_MODEL), F32)
    branches = ((oa_ref[...].astype(BF16), wpa_ref), (ob_ref[...].astype(BF16), wpb_ref), (yc.astype(BF16), wpc_ref))
    for br, (o, wp_ref) in enumerate(branches):
        cols = slice(D_MODEL * br, D_MODEL * (br + 1))
        gate = jax.nn.sigmoid(_nn(h, wg_ref[:, cols]) + bg_ref[:, cols])
        mixed = mixed + gate * _nn(o, wp_ref[...])
    y = _nn(mixed.astype(BF16), wo_ref[...])
    xo_ref[...] = x + _rms(y, gpost_ref[...])

    @pl.when(i % tiles_per_seq == tiles_per_seq - 1)
    def _():
        nb_ref[0] = ext_s[halo + tm - 2 * shift:halo + tm, :]


def _merge(x, oa, ob, gb, u, state_halo, conv_w, g_pre, g_post, w_gate, b_gate, wpa, wpb, wpc, wo,
           *, tm, shift, tiles_per_seq):
    m = x.shape[0]
    halo = state_halo.shape[1]
    n_groups = m // (tm * tiles_per_seq)
    row = lambda i: (i, 0)
    const = lambda i: (0, 0)
    wide = pl.BlockSpec((tm, WIDTH), row)
    return pl.pallas_call(
        functools.partial(_merge_kernel, tm=tm, shift=shift, halo=halo, tiles_per_seq=tiles_per_seq),
        grid=(m // tm,),
        in_specs=[pl.BlockSpec((tm, D_MODEL), row), wide, wide, wide, wide,
                  pl.BlockSpec((halo, WIDTH), lambda i: (jnp.maximum(i * (tm // halo) - 1, 0), 0)),
                  pl.BlockSpec((1, halo, WIDTH), lambda i: (i // tiles_per_seq, 0, 0)),
                  pl.BlockSpec((CONV_W, WIDTH), const),
                  pl.BlockSpec((1, D_MODEL), const), pl.BlockSpec((1, D_MODEL), const),
                  pl.BlockSpec(w_gate.shape, const), pl.BlockSpec((1, N_BRANCH * D_MODEL), const),
                  pl.BlockSpec(wpa.shape, const), pl.BlockSpec(wpb.shape, const),
                  pl.BlockSpec(wpc.shape, const), pl.BlockSpec(wo.shape, const)],
        out_specs=[pl.BlockSpec((tm, D_MODEL), row),
                   pl.BlockSpec((1, 2 * shift, WIDTH), lambda i: (i // tiles_per_seq, 0, 0))],
        out_shape=[jax.ShapeDtypeStruct((m, D_MODEL), F32),
                   jax.ShapeDtypeStruct((n_groups, 2 * shift, WIDTH), F32)],
        scratch_shapes=[pltpu.VMEM((halo + tm, WIDTH), F32)],
        compiler_params=_params(1),
        name="merge",
    )(x, oa, ob, gb, u, u, state_halo, conv_w, g_pre, g_post, w_gate, b_gate, wpa, wpb, wpc, wo)


def _ffn_kernel(x_ref, gpre_ref, gpost_ref, wg_ref, wu_ref, wd_ref, o_ref, h_s, acc_s):
    j = pl.program_id(1)

    @pl.when(j == 0)
    def _():
        h_s[...] = _rms(x_ref[...], gpre_ref[...]).astype(BF16)
        acc_s[...] = jnp.zeros_like(acc_s)

    h = h_s[...]
    g = _nn(h, wg_ref[...])
    up = _nn(h, wu_ref[...])
    act = (g * jax.nn.sigmoid(g) * up).astype(BF16)
    acc_s[...] += _nn(act, wd_ref[...])

    @pl.when(j == pl.num_programs(1) - 1)
    def _():
        o_ref[...] = x_ref[...] + _rms(acc_s[...], gpost_ref[...])


def _ffn(x, g_pre, g_post, wg, wu, wd, *, tm, tf):
    m = x.shape[0]
    row = lambda i, j: (i, 0)
    const = lambda i, j: (0, 0)
    return pl.pallas_call(
        _ffn_kernel,
        grid=(m // tm, D_FF // tf),
        in_specs=[pl.BlockSpec((tm, D_MODEL), row),
                  pl.BlockSpec((1, D_MODEL), const), pl.BlockSpec((1, D_MODEL), const),
                  pl.BlockSpec((D_MODEL, tf), lambda i, j: (0, j)),
                  pl.BlockSpec((D_MODEL, tf), lambda i, j: (0, j)),
                  pl.BlockSpec((tf, D_MODEL), lambda i, j: (j, 0))],
        out_specs=pl.BlockSpec((tm, D_MODEL), row),
        out_shape=jax.ShapeDtypeStruct((m, D_MODEL), F32),
        scratch_shapes=[pltpu.VMEM((tm, D_MODEL), BF16), pltpu.VMEM((tm, D_MODEL), F32)],
        compiler_params=_params(2),
        name="ffn",
    )(x, g_pre, g_post, wg, wu, wd)


def kernel(x_prompt, x_sample, cache_a_k, cache_a_v, cache_a_logf, cache_b_k, cache_b_v, state_conv, page_table, w_in, b_f, conv_w, w_proj_a, w_proj_b, w_proj_c, w_gate, b_gate, w_o, g_mix_pre, g_mix_post, g_ffn_pre, g_ffn_post, w_ffn_gate, w_ffn_up, w_ffn_down):
    n_seq, t_len, _ = x_prompt.shape
    dec_b, dec_t, _ = x_sample.shape
    depth = w_in.shape[0]
    n_pool = cache_a_k.shape[1]
    mp = n_seq * t_len
    ms = dec_b * dec_t
    tq = ATT_BLOCK
    tm_p = 512

    xp = x_prompt.reshape(mp, D_MODEL)
    xs = jnp.transpose(x_sample, (1, 0, 2)).reshape(ms, D_MODEL)
    pt = page_table.reshape(-1)
    feature_major = lambda c: jnp.transpose(c, (0, 1, 3, 4, 2)).reshape(depth, n_pool, WIDTH, PAGE_SIZE)
    caches = (feature_major(cache_a_k), feature_major(cache_a_v), jnp.transpose(cache_a_logf, (0, 1, 3, 2)),
              feature_major(cache_b_k), feature_major(cache_b_v))
    r = lax.broadcasted_iota(jnp.int32, (tq + 2 * SUBLANES, tq), 0)
    c = lax.broadcasted_iota(jnp.int32, (tq + 2 * SUBLANES, tq), 1)
    tri = ((c > r) | (r >= tq)).astype(BF16)
    prompt_halo = jnp.zeros((n_seq, SUBLANES, WIDTH), F32)

    rows_p, rows_s = [], []
    new_kv = ()
    qkv = 3 * WIDTH
    for l in range(depth):
        wl = w_in[l]
        wt = jnp.concatenate([wl[:, 0:qkv], wl[:, qkv + N_HEADS:2 * qkv + N_HEADS]], axis=1).T.astype(BF16)
        wc = wl[:, 2 * qkv + N_HEADS:].astype(BF16)
        w_f = wl[:, qkv:qkv + N_HEADS]
        wf = jnp.pad(w_f, ((0, 0), (0, LANES - N_HEADS))).astype(BF16)
        wft = jnp.pad(w_f.T, ((0, 2 * SUBLANES - N_HEADS), (0, 0))).astype(BF16)
        g_pre, g_post = g_mix_pre[l].reshape(1, -1), g_mix_post[l].reshape(1, -1)
        f_pre, f_post = g_ffn_pre[l].reshape(1, -1), g_ffn_post[l].reshape(1, -1)
        merge_w = (conv_w[l], g_pre, g_post, w_gate[l].astype(BF16), b_gate[l].reshape(1, -1),
                   w_proj_a[l].astype(BF16), w_proj_b[l].astype(BF16), w_proj_c[l].astype(BF16),
                   w_o[l].astype(BF16))
        ffn_w = (f_pre, f_post, w_ffn_gate[l].astype(BF16), w_ffn_up[l].astype(BF16),
                 w_ffn_down[l].astype(BF16))

        (qat, qbt, ka16, kb16, *new_kv, vat16, vbt16, gb, u, logf_t) = _in_proj_prompt(
            l, depth, new_kv, xp, g_pre, wt, wc, wft, b_f[l], n_seq=n_seq, t_len=t_len, tm=tm_p, tq=tq)
        cum, cum_t = _cumsum(logf_t, tq=tq)
        nq = t_len // tq
        oa = _prompt_attention(
            _fox_kernel, "fox_attention", qat, ka16, vat16, (cum, cum_t),
            [pl.BlockSpec((t_len, N_HEADS), lambda b, p: (b, 0)),
             pl.BlockSpec((None, nq, N_HEADS, tq), lambda b, p: (b, 0, 0, 0))],
            [pltpu.VMEM((2, t_len, LANES), F32), pltpu.VMEM((2, nq, tq, tq), F32)], tq=tq)
        ob = _prompt_attention(
            _sb_kernel, "sb_attention", qbt, kb16, vbt16, (tri,),
            [pl.BlockSpec(tri.shape, lambda b, p: (0, 0))],
            [pltpu.VMEM((2, nq, tq, tq), BF16), pltpu.VMEM((2, nq, tq, tq), F32)], tq=tq)
        xp, nb_p = _merge(xp, oa, ob, gb, u, prompt_halo, *merge_w,
                          tm=tm_p, shift=1, tiles_per_seq=t_len // tm_p)
        xp = _ffn(xp, *ffn_w, tm=1024, tf=256)
        rows_p.append((logf_t, nb_p))

        (qa, ka, va, qb, kb, vb, gb, u, logf, logf_t) = _in_proj_sample(
            xs, g_pre, wt, wc, wf, wft, b_f[l])
        oa, ob = _sample_attention(l, pt, qa, qb, ka, va, kb, vb, logf_t, *caches,
                                   dec_b=dec_b, dec_t=dec_t)
        state = jnp.transpose(state_conv[l], (1, 0, 2)).reshape(1, (CONV_W - 1) * dec_b, WIDTH)
        xs, nb_s = _merge(xs, oa, ob, gb, u, state, *merge_w, tm=ms, shift=dec_b, tiles_per_seq=1)
        xs = _ffn(xs, *ffn_w, tm=ms, tf=256)
        rows_s.append((ka, va, logf, kb, vb, nb_s))

    stack = lambda rows, i: jnp.stack([r[i] for r in rows], axis=0)

    def new_kv_p(i):
        a = new_kv[i].reshape(depth, n_seq, N_HEADS, HEAD_DIM, t_len)
        return jnp.transpose(a, (0, 1, 4, 2, 3))

    def stack_s(i, tail):
        a = stack(rows_s, i).reshape(depth, -1, dec_b, *tail)
        return jnp.swapaxes(a, 1, 2)

    heads = (N_HEADS, HEAD_DIM)
    return (xp.reshape(n_seq, t_len, D_MODEL),
            jnp.transpose(xs.reshape(dec_t, dec_b, D_MODEL), (1, 0, 2)),
            new_kv_p(0), new_kv_p(1), jnp.transpose(stack(rows_p, 0), (0, 1, 3, 2)),
            new_kv_p(2), new_kv_p(3), stack(rows_p, 1),
            stack_s(0, heads), stack_s(1, heads), stack_s(2, (N_HEADS,)),
            stack_s(3, heads), stack_s(4, heads), stack_s(5, (WIDTH,)))
```

```python
import functools

import jax
import jax.numpy as jnp
from jax import lax
from jax.experimental import pallas as pl
from jax.experimental.pallas import tpu as pltpu

D_MODEL = 1024
HEAD_DIM = 64
N_HEADS = 8
WIDTH = N_HEADS * HEAD_DIM
CONV_W = 3
D_FF = 2816
N_BRANCH = 3
PAGE_SIZE = 128
RMS_EPS = 1e-6
SCALE = HEAD_DIM ** -0.5
LOG2E = 1.4426950408889634

LANES = 128
SUBLANES = 8
VMEM_LIMIT_BYTES = 52 * 1024 * 1024

BF16 = jnp.bfloat16
F32 = jnp.float32
NEG = -1e30
SIGN_BIT = -2 ** 31

ATT_BLOCK = 256
PAGES_PER_STEP = 16
PAGES_PER_BLOCK = 4


def _nn(a, b):
    return jnp.dot(a, b, preferred_element_type=F32)


def _nt(a, b):
    return lax.dot_general(a, b, (((1,), (1,)), ((), ())), preferred_element_type=F32)


def _rms(x, g):
    ms = jnp.mean(x * x, axis=-1, keepdims=True)
    return x * lax.rsqrt(ms + RMS_EPS) * g


def _log_sigmoid(x):
    return jnp.minimum(x, 0.0) - jnp.log1p(jnp.exp(-jnp.abs(x)))


def _log_sigmoid_pair(z):
    soft = jnp.log1p(jnp.exp(-jnp.abs(z)))
    return jnp.minimum(-z, 0.0) - soft, jnp.minimum(z, 0.0) - soft


def _split_bf16(x, n):
    parts = []
    r = x
    for i in range(n):
        p = r.astype(BF16)
        parts.append(p)
        if i + 1 < n:
            r = r - p.astype(F32)
    return parts


def _params(n_axes):
    return pltpu.CompilerParams(dimension_semantics=("arbitrary",) * n_axes,
                                vmem_limit_bytes=VMEM_LIMIT_BYTES)


def _in_proj_prompt_kernel(x_ref, g_ref, wt_ref, wc_ref, wft_ref, bft_ref, *refs, tq, n_carried):
    (qat_ref, qbt_ref, ka_ref, kb_ref, kat_ref, vat_ref, kbt_ref, vbt_ref,
     vat16_ref, vbt16_ref, gb_ref, u_ref, lft_ref) = refs[n_carried:]
    h = _rms(x_ref[...], g_ref[...]).astype(BF16)
    tm = h.shape[0]
    rows = lambda k: wt_ref[WIDTH * k:WIDTH * (k + 1), :]
    for q_ref, kt_ref, k_ref, vt_ref, vt16_ref, base in ((qat_ref, kat_ref, ka_ref, vat_ref, vat16_ref, 0),
                                                         (qbt_ref, kbt_ref, kb_ref, vbt_ref, vbt16_ref, 3)):
        qt = (_nt(rows(base), h) * (SCALE * LOG2E)).astype(BF16)
        kt_ref[...] = _nt(rows(base + 1), h)
        k_ref[...] = _nt(h, rows(base + 1)).astype(BF16)
        vt = _nt(rows(base + 2), h)
        vt_ref[...] = vt
        for c in range(tm // tq):
            q_ref[c] = qt[:, tq * c:tq * (c + 1)]
            vt16_ref[c] = vt[:, tq * c:tq * (c + 1)].astype(BF16)
    gb_ref[...] = _nn(h, wc_ref[:, 0:WIDTH]).astype(BF16)
    u_ref[...] = _nn(h, wc_ref[:, WIDTH:2 * WIDTH]) * _nn(h, wc_ref[:, 2 * WIDTH:3 * WIDTH])
    ft = _nt(wft_ref[...], h)[:N_HEADS, :]
    lft_ref[...] = _log_sigmoid(ft + bft_ref[...])


N_NEW_KV = 4
FIRST_NEW_KV_OUT = 4


def _in_proj_prompt(layer, depth, carried, x, g, wt, wc, wft, b_f, *, n_seq, t_len, tm, tq):
    m = x.shape[0]
    tps = t_len // tm
    nq = t_len // tq
    row = lambda i: (i, 0)
    const = lambda i: (0, 0)
    tok = lambda dt: (jax.ShapeDtypeStruct((m, WIDTH), dt), pl.BlockSpec((tm, WIDTH), row))
    new_t = (jax.ShapeDtypeStruct((depth, n_seq, WIDTH, t_len), F32),
             pl.BlockSpec((None, None, WIDTH, tm), lambda i: (layer, i // tps, 0, i % tps)))
    blocked = (jax.ShapeDtypeStruct((n_seq, nq, WIDTH, tq), BF16),
               pl.BlockSpec((None, tm // tq, WIDTH, tq), lambda i: (i // tps, i % tps, 0, 0)))
    outs = [blocked, blocked, tok(BF16), tok(BF16)] + [new_t] * N_NEW_KV + [blocked, blocked, tok(BF16), tok(F32),
            (jax.ShapeDtypeStruct((n_seq, N_HEADS, t_len), F32),
             pl.BlockSpec((None, N_HEADS, tm), lambda i: (i // tps, 0, i % tps)))]
    in_specs = [pl.BlockSpec((tm, D_MODEL), row),
                pl.BlockSpec((1, D_MODEL), const),
                pl.BlockSpec(wt.shape, const),
                pl.BlockSpec(wc.shape, const),
                pl.BlockSpec(wft.shape, const),
                pl.BlockSpec((N_HEADS, 1), const)]
    assert len(carried) in (0, N_NEW_KV)
    aliases = {len(in_specs) + k: FIRST_NEW_KV_OUT + k for k in range(len(carried))}
    return pl.pallas_call(
        functools.partial(_in_proj_prompt_kernel, tq=tq, n_carried=len(carried)),
        grid=(m // tm,),
        in_specs=in_specs + [pl.BlockSpec(memory_space=pl.ANY)] * len(carried),
        out_specs=[o[1] for o in outs],
        out_shape=[o[0] for o in outs],
        input_output_aliases=aliases,
        compiler_params=_params(1),
        name="in_proj_prompt",
    )(x, g, wt, wc, wft, b_f.reshape(N_HEADS, 1), *carried)


def _in_proj_sample_kernel(x_ref, g_ref, wt_ref, wc_ref, wf_ref, wft_ref, bf_ref, bft_ref,
                           qa_ref, ka_ref, va_ref, qb_ref, kb_ref, vb_ref, gb_ref, u_ref, lf_ref, lft_ref):
    h = _rms(x_ref[...], g_ref[...]).astype(BF16)
    for k, o_ref in enumerate((qa_ref, ka_ref, va_ref, qb_ref, kb_ref, vb_ref)):
        y = _nt(h, wt_ref[WIDTH * k:WIDTH * (k + 1), :])
        o_ref[...] = y * SCALE if k % 3 == 0 else y
    gb_ref[...] = _nn(h, wc_ref[:, 0:WIDTH]).astype(BF16)
    u_ref[...] = _nn(h, wc_ref[:, WIDTH:2 * WIDTH]) * _nn(h, wc_ref[:, 2 * WIDTH:3 * WIDTH])
    f = _nn(h, wf_ref[...])[:, :N_HEADS]
    lf_ref[...] = _log_sigmoid(f + bf_ref[...])
    ft = _nt(wft_ref[...], h)[:N_HEADS, :]
    lft_ref[...] = _log_sigmoid(ft + bft_ref[...])


def _in_proj_sample(x, g, wt, wc, wf, wft, b_f):
    m = x.shape[0]
    full = lambda shape: pl.BlockSpec(shape, lambda i: (0,) * len(shape))
    tok = lambda dt: jax.ShapeDtypeStruct((m, WIDTH), dt)
    out_shape = [tok(F32)] * 6 + [tok(BF16), tok(F32),
                                  jax.ShapeDtypeStruct((m, N_HEADS), F32), jax.ShapeDtypeStruct((N_HEADS, m), F32)]
    args = (x, g, wt, wc, wf, wft, b_f.reshape(1, N_HEADS), b_f.reshape(N_HEADS, 1))
    return pl.pallas_call(
        _in_proj_sample_kernel,
        grid=(1,),
        in_specs=[full(a.shape) for a in args],
        out_specs=[full(s.shape) for s in out_shape],
        out_shape=out_shape,
        compiler_params=_params(1),
        name="in_proj_sample",
    )(*args)


def _cumsum_kernel(lt_ref, c_ref, ct_ref, *, t_len, tq):
    nb = t_len // LANES
    x = jnp.concatenate([lt_ref[:, LANES * k:LANES * (k + 1)] for k in range(nb)], axis=0)
    n = nb * N_HEADS
    r = lax.broadcasted_iota(jnp.int32, (LANES, LANES), 0)
    c = lax.broadcasted_iota(jnp.int32, (LANES, LANES), 1)
    incl = (r <= c).astype(BF16)
    y = sum(_nn(p, incl) for p in _split_bf16(x, 3))
    r = lax.broadcasted_iota(jnp.int32, (n, n), 0)
    c = lax.broadcasted_iota(jnp.int32, (n, n), 1)
    earlier = ((c // N_HEADS < r // N_HEADS) & (c % N_HEADS == r % N_HEADS)).astype(BF16)
    offs = sum(_nn(earlier, p) for p in _split_bf16(y, 3))[:, LANES - 1:LANES]
    z = (y + offs) * LOG2E
    zt = z.T
    per_q = tq // LANES
    for k in range(nb):
        ct_ref[k // per_q, :, LANES * (k % per_q):LANES * (k % per_q + 1)] = z[N_HEADS * k:N_HEADS * (k + 1), :]
        c_ref[LANES * k:LANES * (k + 1), :] = zt[:, N_HEADS * k:N_HEADS * (k + 1)]


def _cumsum(logf_t, *, tq):
    n_seq, _, t_len = logf_t.shape
    nq = t_len // tq
    return pl.pallas_call(
        functools.partial(_cumsum_kernel, t_len=t_len, tq=tq),
        grid=(n_seq,),
        in_specs=[pl.BlockSpec((None, N_HEADS, t_len), lambda b: (b, 0, 0))],
        out_specs=[pl.BlockSpec((t_len, N_HEADS), lambda b: (b, 0)),
                   pl.BlockSpec((None, nq, N_HEADS, tq), lambda b: (b, 0, 0, 0))],
        out_shape=[jax.ShapeDtypeStruct((n_seq * t_len, N_HEADS), F32),
                   jax.ShapeDtypeStruct((n_seq, nq, N_HEADS, tq), F32)],
        compiler_params=_params(1),
        name="logf_cumsum",
    )(logf_t)


def _head_pair_qt(qt):
    low =lax.broadcasted_iota(jnp.int32, qt.shape, 0) < HEAD_DIM
    zero = jnp.zeros_like(qt)
    return jnp.where(low, qt, zero), jnp.where(low, zero, qt)


def _head_column(c, head):
    hcol = lax.broadcasted_iota(jnp.int32, c.shape, 1)
    return jnp.sum(jnp.where(hcol == head, c, 0.0), axis=1, keepdims=True)


def _head_row(ct, head):
    hrow = lax.broadcasted_iota(jnp.int32, ct.shape, 0)
    return jnp.sum(jnp.where(hrow == head, ct, 0.0), axis=0, keepdims=True)


def _fox_kernel(qt_ref, k_ref, vt_ref, c_ref, ct_ref, o_ref, ck_s, s_s, *, tq):
    pair = pl.program_id(1)
    c_all = c_ref[...]
    for e in (0, 1):
        ck_s[e] = jnp.broadcast_to(_head_column(c_all, 2 * pair + e), ck_s.shape[1:])

    def query_block(qi, carry):
        o_ref[pl.ds(pl.multiple_of(qi * tq, tq), tq), :] = _fox_query_block(
            qi, pair, qt_ref[qi], ct_ref[qi], k_ref, vt_ref, ck_s, s_s, tq).astype(o_ref.dtype)
        return carry

    lax.fori_loop(0, qt_ref.shape[0], query_block, 0)


def _fox_query_block(qi, pair, qt, cq_all, k_ref, vt_ref, ck_s, s_s, tq):
    qts = _head_pair_qt(qt)
    cq = [_head_row(cq_all, 2 * pair + e) for e in (0, 1)]
    key = lax.broadcasted_iota(jnp.int32, (tq, tq), 0)
    qry = lax.broadcasted_iota(jnp.int32, (tq, tq), 1)
    reps = tq // LANES
    n_full = qi // 2

    def logits(js, m, masked):
        starts = [pl.multiple_of(j * tq, tq) for j in js]
        qk = [[_nn(k_ref[pl.ds(st, tq), :], qts[e]) for e in (0, 1)] for st in starts]
        m = list(m)
        for b, (j, st) in enumerate(zip(js, starts)):
            for e in (0, 1):
                ck = ck_s[e, pl.ds(st, tq), :]
                s = qk[b][e] + (cq[e] - jnp.concatenate([ck] * reps, axis=1))
                if masked:
                    s = jnp.where(key + (j - qi) * tq <= qry, s, NEG)
                s_s[e, j] = s
                m[e] = jnp.maximum(m[e], jnp.max(s, axis=0, keepdims=True))
        return tuple(m)

    even = qi % 2 == 0
    m = lax.fori_loop(0, n_full, lambda i, m: logits((2 * i, 2 * i + 1), m, False),
                      (jnp.full((1, tq), NEG, F32),) * 2)
    m = lax.cond(even, lambda m: logits((qi,), m, True), lambda m: logits((qi - 1, qi), m, True), m)

    ones = jnp.ones((2 * SUBLANES, tq), BF16)

    def weigh(js, acc):
        p = [[jnp.exp2(s_s[e, j] - m[e]).astype(BF16) for e in (0, 1)] for j in js]
        pv = [[_nn(jnp.concatenate([vt_ref[j, HEAD_DIM * e:HEAD_DIM * (e + 1), :], ones], axis=0), p[b][e])
               for e in (0, 1)] for b, j in enumerate(js)]
        return tuple(acc[e] + sum(x[e] for x in pv) for e in (0, 1))

    acc = lax.fori_loop(0, n_full, lambda i, acc: weigh((2 * i, 2 * i + 1), acc),
                        (jnp.zeros((HEAD_DIM + 2 * SUBLANES, tq), F32),) * 2)
    acc = lax.cond(even, lambda acc: weigh((qi,), acc), lambda acc: weigh((qi - 1, qi), acc), acc)
    out = [acc[e][0:HEAD_DIM] / acc[e][HEAD_DIM:HEAD_DIM + 1] for e in (0, 1)]
    return jnp.concatenate(out, axis=0).T


def _sb_kernel(qt_ref, k_ref, vt_ref, tri_ref, o_ref, lr_s, pos_s, *, tq):
    def query_block(qi, carry):
        o_ref[pl.ds(pl.multiple_of(qi * tq, tq), tq), :] = _sb_query_block(
            qi, qt_ref[qi], k_ref, vt_ref, tri_ref, lr_s, pos_s, tq).astype(o_ref.dtype)
        return carry

    lax.fori_loop(0, qt_ref.shape[0], query_block, 0)


def _sb_query_block(qi, qt, k_ref, vt_ref, tri_ref, lr_s, pos_s, tq):
    nqts = [-x for x in _head_pair_qt(qt)]
    key = lax.broadcasted_iota(jnp.int32, (tq, tq), 0)
    qry = lax.broadcasted_iota(jnp.int32, (tq, tq), 1)
    n_full = qi // 2

    def gates(js, masked):
        nz = [[_nn(k_ref[pl.ds(pl.multiple_of(j * tq, tq), tq), :], nqts[e]) for e in (0, 1)] for j in js]
        for b, j in enumerate(js):
            for e in (0, 1):
                n = nz[b][e]
                neg_abs = lax.bitcast_convert_type(lax.bitcast_convert_type(n, jnp.int32) | SIGN_BIT, F32)
                lr = jnp.minimum(n, 0.0) - jnp.log2(1.0 + jnp.exp2(neg_abs))
                ls_pos = lr - n
                if masked:
                    strict = key + (j - qi) * tq < qry
                    lr = jnp.where(strict, lr, 0.0)
                    ls_pos = jnp.where(strict, ls_pos, NEG)
                lr_s[e, j] = lr.astype(BF16)
                pos_s[e, j] = ls_pos

    def unmasked(i, carry):
        gates((2 * i, 2 * i + 1), False)
        return carry

    even = qi % 2 == 0
    lax.fori_loop(0, n_full, unmasked, 0)

    @pl.when(even)
    def _():
        gates((qi,), True)

    @pl.when(jnp.logical_not(even))
    def _():
        gates((qi - 1, qi), True)

    tri = tri_ref[...]

    def weigh(js, carry):
        sums = [[_nn(tri, lr_s[e, j]) for e in (0, 1)] for j in js]
        rest =[[carry[e][0]] for e in (0, 1)]
        for b in range(len(js)):
            for e in (0, 1):
                rest[e].append(rest[e][b] + sums[b][e][tq:tq + 1])
        a = [[jnp.exp2(pos_s[e, j] + (sums[b][e][0:tq] + rest[e][b])).astype(BF16) for e in (0, 1)]
             for b, j in enumerate(js)]
        pv = [[_nn(vt_ref[j, HEAD_DIM * e:HEAD_DIM * (e + 1), :], a[b][e]) for e in (0, 1)]
              for b, j in enumerate(js)]
        return tuple((rest[e][-1], carry[e][1] + sum(x[e] for x in pv)) for e in (0, 1))

    init = ((jnp.zeros((1, tq), F32), jnp.zeros((HEAD_DIM, tq), F32)),) * 2
    carry = lax.cond(even, lambda c: weigh((qi,), c), lambda c: weigh((qi, qi - 1), c), init)
    top = 2 * n_full - 1
    (_, a0), (_, a1) = lax.fori_loop(0, n_full, lambda i, c: weigh((top - 2 * i, top - 2 * i - 1), c), carry)
    return jnp.concatenate([a0, a1], axis=0).T


def _prompt_attention(kernel_fn, name, qt16, k, vt16, extra, extra_specs, scratch, *, tq):
    n_seq, nq = vt16.shape[:2]
    t_len = nq * tq
    n_pair = WIDTH // LANES
    blocked = pl.BlockSpec((None, nq, LANES, tq), lambda b, p: (b, 0, p, 0))
    tokens = pl.BlockSpec((t_len, LANES), lambda b, p: (b, p))
    return pl.pallas_call(
        functools.partial(kernel_fn, tq=tq),
        grid=(n_seq, n_pair),
        in_specs=[blocked, tokens, blocked] + extra_specs,
        out_specs=tokens,
        out_shape=jax.ShapeDtypeStruct(k.shape, BF16),
        scratch_shapes=scratch,
        compiler_params=_params(2),
        name=name,
    )(qt16, k, vt16, *extra)


def _sample_attn_kernel(pt_ref, qa_ref, qb_ref, kan_ref, van_ref, kbn_ref, vbn_ref, lfn_ref,
                        tri_ref, hm_ref, *rest, n_pg, dec_b, dec_t):
    del pt_ref
    pg_ak, pg_av, pg_bk, pg_bv, pg_lf = (rest[n_pg * i:n_pg * (i + 1)] for i in range(5))
    oa_ref, ob_ref = rest[5 * n_pg:5 * n_pg + 2]
    qa_s, qb_s, acca_s, accb_s, m_s, l_s, dec_s, rest_s = rest[5 * n_pg + 2:]
    b = pl.program_id(0)
    j = pl.program_id(1)
    nr = dec_t * N_HEADS
    hm = hm_ref[...]

    def token_rows(ref):
        return [ref[pl.ds(t * dec_b + b, 1), :] for t in range(dec_t)]

    def bf16_exact(x):
        return x.astype(BF16).astype(F32)

    def lanes(xs):
        return xs[0] if len(xs) == 1 else jnp.concatenate(xs, axis=1)

    def update(blocks, own):
        n = PAGE_SIZE * len(blocks[0][0])
        rowsum = lambda x: jnp.sum(x, axis=1, keepdims=True)
        sa = [_nn(qa_s[...], lanes(blk[0])) for blk in blocks]
        sb = [_nn(qb_s[...], lanes(blk[2])) for blk in blocks]
        lf = [jnp.concatenate([lanes(blk[4])] * dec_t, axis=0) for blk in blocks]
        ls = [_log_sigmoid_pair(x) for x in sb]
        if own:
            key = lax.broadcasted_iota(jnp.int32, (nr, n), 1)
            t_row = lax.broadcasted_iota(jnp.int32, (nr, n), 0) // N_HEADS
            valid_a = key <= t_row
            valid_b = key < t_row
            lr = [jnp.where(valid_b, x[0], 0.0) for x in ls]
        else:
            lr = [x[0] for x in ls]
        tri = tri_ref[0:n, 0:n]
        later = [_nn(jnp.concatenate(_split_bf16(f, 2) + _split_bf16(r, 2), axis=0), tri)
                 for f, r in zip(lf, lr)]
        later_a = [x[0:nr] + x[nr:2 * nr] for x in later]
        later_b = [x[2 * nr:3 * nr] + x[3 * nr:4 * nr] for x in later]

        dec = [-rowsum(jnp.where(key == t_row, later_a[0], 0.0)) if own else dec_s[...]]
        rest = [rest_s[...]]
        for f, r in zip(lf, lr):
            dec.append(dec[-1] + rowsum(f))
            rest.append(rest[-1] + rowsum(r))
        s = [x + (la + d) for x, la, d in zip(sa, later_a, dec)]
        if own:
            s = [jnp.where(valid_a, x, NEG) for x in s]
        m_old = m_s[...]
        m_new = functools.reduce(jnp.maximum, [m_old] + [jnp.max(x, axis=1, keepdims=True) for x in s])
        alpha = jnp.exp(m_old - m_new)
        p = [jnp.exp(x - m_new) for x in s]
        a = [jnp.exp(x[1] + lb + r) for x, lb, r in zip(ls, later_b, rest)]
        if own:
            a = [jnp.where(valid_b, x, 0.0) for x in a]
        pv_a = sum(_nt(bf16_exact(x), lanes(blk[1])) for x, blk in zip(p, blocks))
        pv_b = sum(_nt(bf16_exact(x), lanes(blk[3])) for x, blk in zip(a, blocks))
        l_s[...] = alpha * l_s[...] + sum(rowsum(x) for x in p)
        acca_s[...] = alpha * acca_s[...] + pv_a
        m_s[...] = m_new
        dec_s[...] = dec[-1]
        accb_s[...] = accb_s[...] + pv_b
        rest_s[...] = rest[-1]

    @pl.when(j == 0)
    def _():
        def block_diag_q(ref):
            return bf16_exact(jnp.concatenate(
                [jnp.broadcast_to(r, (N_HEADS, WIDTH)) * hm for r in token_rows(ref)], axis=0))

        def own_tile(ref):
            sub = lax.broadcasted_iota(jnp.int32, (SUBLANES, WIDTH), 0)
            top = jnp.zeros((SUBLANES, WIDTH), F32)
            for t, r in enumerate(token_rows(ref)):
                top = jnp.where(sub == t, jnp.broadcast_to(r, (SUBLANES, WIDTH)), top)
            return jnp.concatenate([top, jnp.zeros((PAGE_SIZE - SUBLANES, WIDTH), F32)], axis=0).T

        qa_s[...] = block_diag_q(qa_ref)
        qb_s[...] = block_diag_q(qb_ref)
        lane = lax.broadcasted_iota(jnp.int32, (N_HEADS, dec_b * dec_t), 1)
        pos = lax.broadcasted_iota(jnp.int32, (N_HEADS, PAGE_SIZE), 1)
        lfn = lfn_ref[...]
        lf0 = jnp.zeros((N_HEADS, PAGE_SIZE), F32)
        for t in range(dec_t):
            col = jnp.sum(jnp.where(lane == t * dec_b + b, lfn, 0.0), axis=1, keepdims=True)
            lf0 = jnp.where(pos == t, col, lf0)
        acca_s[...] = jnp.zeros((nr, WIDTH), F32)
        accb_s[...] = jnp.zeros((nr, WIDTH), F32)
        m_s[...] = jnp.full((nr, 1), NEG, F32)
        l_s[...] = jnp.zeros((nr, 1), F32)
        dec_s[...] = jnp.zeros((nr, 1), F32)
        rest_s[...] = jnp.zeros((nr, 1), F32)
        update([([own_tile(kan_ref)], [own_tile(van_ref)], [own_tile(kbn_ref)], [own_tile(vbn_ref)], [lf0])], True)

    @pl.when(j > 0)
    def _():
        update([[[pg[...] for pg in pgs[hi - PAGES_PER_BLOCK:hi]] for pgs in (pg_ak, pg_av, pg_bk, pg_bv, pg_lf)]
                for hi in range(n_pg, 0, -PAGES_PER_BLOCK)], False)

    @pl.when(j == pl.num_programs(1) - 1)
    def _():
        oa = acca_s[...] / l_s[...]
        ob = accb_s[...]
        for t in range(dec_t):
            rows = slice(N_HEADS * t, N_HEADS * (t + 1))
            oa_ref[pl.ds(t * dec_b + b, 1), :] = jnp.sum(oa[rows] * hm, axis=0, keepdims=True)
            ob_ref[pl.ds(t * dec_b + b, 1), :] = jnp.sum(ob[rows] * hm, axis=0, keepdims=True)


def _sample_attention(layer, page_table, qa, qb, kan, van, kbn, vbn, lfn_t,
                      cache_a_kt, cache_a_vt, cache_a_lft, cache_b_kt, cache_b_vt, *, dec_b, dec_t):
    n_pg = PAGES_PER_STEP
    n_pages = page_table.shape[0] // dec_b
    nk = PAGES_PER_BLOCK * PAGE_SIZE
    nr = dec_t * N_HEADS
    m = dec_b * dec_t
    r = lax.broadcasted_iota(jnp.int32, (nk, nk), 0)
    c = lax.broadcasted_iota(jnp.int32, (nk, nk), 1)
    tri = (r > c).astype(BF16)
    hm = (lax.broadcasted_iota(jnp.int32, (N_HEADS, WIDTH), 1) // HEAD_DIM
          == lax.broadcasted_iota(jnp.int32, (N_HEADS, WIDTH), 0)).astype(F32)

    def page_spec(i, rows):
        def index(b, j, pt):
            pg = n_pages - jnp.maximum(j, 1) * n_pg + i
            return (layer, pt[b * n_pages + pg], 0, 0)
        return pl.BlockSpec((None, None, rows, PAGE_SIZE), index)

    full = lambda shape: pl.BlockSpec(shape, lambda b, j, pt: (0,) * len(shape))
    caches = (cache_a_kt, cache_a_vt, cache_b_kt, cache_b_vt, cache_a_lft)
    heights = (WIDTH, WIDTH, WIDTH, WIDTH, N_HEADS)
    page_specs = [page_spec(i, h) for h in heights for i in range(n_pg)]
    page_args = [cch for cch in caches for _ in range(n_pg)]
    grid_spec = pltpu.PrefetchScalarGridSpec(
        num_scalar_prefetch=1,
        grid=(dec_b, 1 + n_pages // n_pg),
        in_specs=[full((m, WIDTH))] * 6 + [full((N_HEADS, m)), full((nk, nk)), full((N_HEADS, WIDTH))]
        + page_specs,
        out_specs=[full((m, WIDTH)), full((m, WIDTH))],
        scratch_shapes=[pltpu.VMEM((nr, WIDTH), F32)] * 4 + [pltpu.VMEM((nr, 1), F32)] * 4,
    )
    return pl.pallas_call(
        functools.partial(_sample_attn_kernel, n_pg=n_pg, dec_b=dec_b, dec_t=dec_t),
        grid_spec=grid_spec,
        out_shape=[jax.ShapeDtypeStruct((m, WIDTH), F32)] * 2,
        compiler_params=_params(2),
        name="sample_attention",
    )(page_table, qa, qb, kan, van, kbn, vbn, lfn_t, tri, hm, *page_args)


def _merge_kernel(x_ref, oa_ref, ob_ref, gb_ref, u_ref, uprev_ref, st_ref, cw_ref, gpre_ref, gpost_ref,
                  wg_ref, bg_ref, wpa_ref, wpb_ref, wpc_ref, wo_ref, xo_ref, nb_ref, ext_s,
                  *, tm, shift, halo, tiles_per_seq):
    i = pl.program_id(0)
    x = x_ref[...]
    h = _rms(x, gpre_ref[...]).astype(BF16)
    u = u_ref[...]
    seq_start = i % tiles_per_seq == 0
    ext_s[0:halo, :] = jnp.where(seq_start, st_ref[0], uprev_ref[...])
    ext_s[halo:halo + tm, :] = u
    conv = (cw_ref[0:1, :] * ext_s[halo - 2 * shift:halo - 2 * shift + tm, :]
            + cw_ref[1:2, :] * ext_s[halo - shift:halo - shift + tm, :]
            + cw_ref[2:3, :] * u)
    yc = gb_ref[...].astype(F32) * conv
    mixed = jnp.zeros((tm, D_MODEL), F32)
    branches = ((oa_ref[...].astype(BF16), wpa_ref), (ob_ref[...].astype(BF16), wpb_ref), (yc.astype(BF16), wpc_ref))
    for br, (o, wp_ref) in enumerate(branches):
        cols = slice(D_MODEL * br, D_MODEL * (br + 1))
        gate = jax.nn.sigmoid(_nn(h, wg_ref[:, cols]) + bg_ref[:, cols])
        mixed = mixed + gate * _nn(o, wp_ref[...])
    y = _nn(mixed.astype(BF16), wo_ref[...])
    xo_ref[...] = x + _rms(y, gpost_ref[...])

    @pl.when(i % tiles_per_seq == tiles_per_seq - 1)
    def _():
        nb_ref[0] = ext_s[halo + tm - 2 * shift:halo + tm, :]


def _merge(x, oa, ob, gb, u, state_halo, conv_w, g_pre, g_post, w_gate, b_gate, wpa, wpb, wpc, wo,
           *, tm, shift, tiles_per_seq):
    m = x.shape[0]
    halo = state_halo.shape[1]
    n_groups = m // (tm * tiles_per_seq)
    row = lambda i: (i, 0)
    const = lambda i: (0, 0)
    wide = pl.BlockSpec((tm, WIDTH), row)
    return pl.pallas_call(
        functools.partial(_merge_kernel, tm=tm, shift=shift, halo=halo, tiles_per_seq=tiles_per_seq),
        grid=(m // tm,),
        in_specs=[pl.BlockSpec((tm, D_MODEL), row), wide, wide, wide, wide,
                  pl.BlockSpec((halo, WIDTH), lambda i: (jnp.maximum(i * (tm // halo) - 1, 0), 0)),
                  pl.BlockSpec((1, halo, WIDTH), lambda i: (i // tiles_per_seq, 0, 0)),
                  pl.BlockSpec((CONV_W, WIDTH), const),
                  pl.BlockSpec((1, D_MODEL), const), pl.BlockSpec((1, D_MODEL), const),
                  pl.BlockSpec(w_gate.shape, const), pl.BlockSpec((1, N_BRANCH * D_MODEL), const),
                  pl.BlockSpec(wpa.shape, const), pl.BlockSpec(wpb.shape, const),
                  pl.BlockSpec(wpc.shape, const), pl.BlockSpec(wo.shape, const)],
        out_specs=[pl.BlockSpec((tm, D_MODEL), row),
                   pl.BlockSpec((1, 2 * shift, WIDTH), lambda i: (i // tiles_per_seq, 0, 0))],
        out_shape=[jax.ShapeDtypeStruct((m, D_MODEL), F32),
                   jax.ShapeDtypeStruct((n_groups, 2 * shift, WIDTH), F32)],
        scratch_shapes=[pltpu.VMEM((halo + tm, WIDTH), F32)],
        compiler_params=_params(1),
        name="merge",
    )(x, oa, ob, gb, u, u, state_halo, conv_w, g_pre, g_post, w_gate, b_gate, wpa, wpb, wpc, wo)


def _ffn_kernel(x_ref, gpre_ref, gpost_ref, wg_ref, wu_ref, wd_ref, o_ref, h_s, acc_s):
    j = pl.program_id(1)

    @pl.when(j == 0)
    def _():
        h_s[...] = _rms(x_ref[...], gpre_ref[...]).astype(BF16)
        acc_s[...] = jnp.zeros_like(acc_s)

    h = h_s[...]
    g = _nn(h, wg_ref[...])
    up = _nn(h, wu_ref[...])
    act = (g * jax.nn.sigmoid(g) * up).astype(BF16)
    acc_s[...] += _nn(act, wd_ref[...])

    @pl.when(j == pl.num_programs(1) - 1)
    def _():
        o_ref[...] = x_ref[...] + _rms(acc_s[...], gpost_ref[...])


def _ffn(x, g_pre, g_post, wg, wu, wd, *, tm, tf):
    m = x.shape[0]
    row = lambda i, j: (i, 0)
    const = lambda i, j: (0, 0)
    return pl.pallas_call(
        _ffn_kernel,
        grid=(m // tm, D_FF // tf),
        in_specs=[pl.BlockSpec((tm, D_MODEL), row),
                  pl.BlockSpec((1, D_MODEL), const), pl.BlockSpec((1, D_MODEL), const),
                  pl.BlockSpec((D_MODEL, tf), lambda i, j: (0, j)),
                  pl.BlockSpec((D_MODEL, tf), lambda i, j: (0, j)),
                  pl.BlockSpec((tf, D_MODEL), lambda i, j: (j, 0))],
        out_specs=pl.BlockSpec((tm, D_MODEL), row),
        out_shape=jax.ShapeDtypeStruct((m, D_MODEL), F32),
        scratch_shapes=[pltpu.VMEM((tm, D_MODEL), BF16), pltpu.VMEM((tm, D_MODEL), F32)],
        compiler_params=_params(2),
        name="ffn",
    )(x, g_pre, g_post, wg, wu, wd)


def kernel(x_prompt, x_sample, cache_a_k, cache_a_v, cache_a_logf, cache_b_k, cache_b_v, state_conv, page_table, w_in, b_f, conv_w, w_proj_a, w_proj_b, w_proj_c, w_gate, b_gate, w_o, g_mix_pre, g_mix_post, g_ffn_pre, g_ffn_post, w_ffn_gate, w_ffn_up, w_ffn_down):
    n_seq, t_len, _ = x_prompt.shape
    dec_b, dec_t, _ = x_sample.shape
    depth = w_in.shape[0]
    n_pool = cache_a_k.shape[1]
    mp = n_seq * t_len
    ms = dec_b * dec_t
    tq = ATT_BLOCK
    tm_p = 512

    xp = x_prompt.reshape(mp, D_MODEL)
    xs = jnp.transpose(x_sample, (1, 0, 2)).reshape(ms, D_MODEL)
    pt = page_table.reshape(-1)
    feature_major = lambda c: jnp.transpose(c, (0, 1, 3, 4, 2)).reshape(depth, n_pool, WIDTH, PAGE_SIZE)
    caches = (feature_major(cache_a_k), feature_major(cache_a_v), jnp.transpose(cache_a_logf, (0, 1, 3, 2)),
              feature_major(cache_b_k), feature_major(cache_b_v))
    r = lax.broadcasted_iota(jnp.int32, (tq + 2 * SUBLANES, tq), 0)
    c = lax.broadcasted_iota(jnp.int32, (tq + 2 * SUBLANES, tq), 1)
    tri = ((c > r) | (r >= tq)).astype(BF16)
    prompt_halo = jnp.zeros((n_seq, SUBLANES, WIDTH), F32)

    rows_p, rows_s = [], []
    new_kv = ()
    qkv = 3 * WIDTH
    for l in range(depth):
        wl = w_in[l]
        wt = jnp.concatenate([wl[:, 0:qkv], wl[:, qkv + N_HEADS:2 * qkv + N_HEADS]], axis=1).T.astype(BF16)
        wc = wl[:, 2 * qkv + N_HEADS:].astype(BF16)
        w_f = wl[:, qkv:qkv + N_HEADS]
        wf = jnp.pad(w_f, ((0, 0), (0, LANES - N_HEADS))).astype(BF16)
        wft = jnp.pad(w_f.T, ((0, 2 * SUBLANES - N_HEADS), (0, 0))).astype(BF16)
        g_pre, g_post = g_mix_pre[l].reshape(1, -1), g_mix_post[l].reshape(1, -1)
        f_pre, f_post = g_ffn_pre[l].reshape(1, -1), g_ffn_post[l].reshape(1, -1)
        merge_w = (conv_w[l], g_pre, g_post, w_gate[l].astype(BF16), b_gate[l].reshape(1, -1),
                   w_proj_a[l].astype(BF16), w_proj_b[l].astype(BF16), w_proj_c[l].astype(BF16),
                   w_o[l].astype(BF16))
        ffn_w = (f_pre, f_post, w_ffn_gate[l].astype(BF16), w_ffn_up[l].astype(BF16),
                 w_ffn_down[l].astype(BF16))

        (qat, qbt, ka16, kb16, *new_kv, vat16, vbt16, gb, u, logf_t) = _in_proj_prompt(
            l, depth, new_kv, xp, g_pre, wt, wc, wft, b_f[l], n_seq=n_seq, t_len=t_len, tm=tm_p, tq=tq)
        cum, cum_t = _cumsum(logf_t, tq=tq)
        nq = t_len // tq
        oa = _prompt_attention(
            _fox_kernel, "fox_attention", qat, ka16, vat16, (cum, cum_t),
            [pl.BlockSpec((t_len, N_HEADS), lambda b, p: (b, 0)),
             pl.BlockSpec((None, nq, N_HEADS, tq), lambda b, p: (b, 0, 0, 0))],
            [pltpu.VMEM((2, t_len, LANES), F32), pltpu.VMEM((2, nq, tq, tq), F32)], tq=tq)
        ob = _prompt_attention(
            _sb_kernel, "sb_attention", qbt, kb16, vbt16, (tri,),
            [pl.BlockSpec(tri.shape, lambda b, p: (0, 0))],
            [pltpu.VMEM((2, nq, tq, tq), BF16), pltpu.VMEM((2, nq, tq, tq), F32)], tq=tq)
        xp, nb_p = _merge(xp, oa, ob, gb, u, prompt_halo, *merge_w,
                          tm=tm_p, shift=1, tiles_per_seq=t_len // tm_p)
        xp = _ffn(xp, *ffn_w, tm=1024, tf=256)
        rows_p.append((logf_t, nb_p))

        (qa, ka, va, qb, kb, vb, gb, u, logf, logf_t) = _in_proj_sample(
            xs, g_pre, wt, wc, wf, wft, b_f[l])
        oa, ob = _sample_attention(l, pt, qa, qb, ka, va, kb, vb, logf_t, *caches,
                                   dec_b=dec_b, dec_t=dec_t)
        state = jnp.transpose(state_conv[l], (1, 0, 2)).reshape(1, (CONV_W - 1) * dec_b, WIDTH)
        xs, nb_s = _merge(xs, oa, ob, gb, u, state, *merge_w, tm=ms, shift=dec_b, tiles_per_seq=1)
        xs = _ffn(xs, *ffn_w, tm=ms, tf=256)
        rows_s.append((ka, va, logf, kb, vb, nb_s))

    stack = lambda rows, i: jnp.stack([r[i] for r in rows], axis=0)

    def new_kv_p(i):
        a = new_kv[i].reshape(depth, n_seq, N_HEADS, HEAD_DIM, t_len)
        return jnp.transpose(a, (0, 1, 4, 2, 3))

    def stack_s(i, tail):
        a = stack(rows_s, i).reshape(depth, -1, dec_b, *tail)
        return jnp.swapaxes(a, 1, 2)

    heads = (N_HEADS, HEAD_DIM)
    return (xp.reshape(n_seq, t_len, D_MODEL),
            jnp.transpose(xs.reshape(dec_t, dec_b, D_MODEL), (1, 0, 2)),
            new_kv_p(0), new_kv_p(1), jnp.transpose(stack(rows_p, 0), (0, 1, 3, 2)),
            new_kv_p(2), new_kv_p(3), stack(rows_p, 1),
            stack_s(0, heads), stack_s(1, heads), stack_s(2, (N_HEADS,)),
            stack_s(3, heads), stack_s(4, heads), stack_s(5, (WIDTH,)))
```

```python
import functools

import jax
import jax.numpy as jnp
from jax import lax
from jax.experimental import pallas as pl
from jax.experimental.pallas import tpu as pltpu

D_MODEL = 1024
HEAD_DIM = 64
N_HEADS = 8
WIDTH = N_HEADS * HEAD_DIM
CONV_W = 3
D_FF = 2816
N_BRANCH = 3
PAGE_SIZE = 128
RMS_EPS = 1e-6
SCALE = HEAD_DIM ** -0.5
LOG2E = 1.4426950408889634

LANES = 128
SUBLANES = 8
VMEM_LIMIT_BYTES = 52 * 1024 * 1024

BF16 = jnp.bfloat16
F32 = jnp.float32
NEG = -1e30
SIGN_BIT = -2 ** 31

ATT_BLOCK = 256
PAGES_PER_STEP = 16
PAGES_PER_BLOCK = 4


def _nn(a, b):
    return jnp.dot(a, b, preferred_element_type=F32)


def _nt(a, b):
    return lax.dot_general(a, b, (((1,), (1,)), ((), ())), preferred_element_type=F32)


def _rms(x, g):
    ms = jnp.mean(x * x, axis=-1, keepdims=True)
    return x * lax.rsqrt(ms + RMS_EPS) * g


def _log_sigmoid(x):
    return jnp.minimum(x, 0.0) - jnp.log1p(jnp.exp(-jnp.abs(x)))


def _log_sigmoid_pair(z):
    soft = jnp.log1p(jnp.exp(-jnp.abs(z)))
    return jnp.minimum(-z, 0.0) - soft, jnp.minimum(z, 0.0) - soft


def _split_bf16(x, n):
    parts = []
    r = x
    for i in range(n):
        p = r.astype(BF16)
        parts.append(p)
        if i + 1 < n:
            r = r - p.astype(F32)
    return parts


def _params(n_axes):
    return pltpu.CompilerParams(dimension_semantics=("arbitrary",) * n_axes,
                                vmem_limit_bytes=VMEM_LIMIT_BYTES)


def _in_proj_prompt_kernel(x_ref, g_ref, wt_ref, wc_ref, wft_ref, bft_ref, *refs, tq, n_carried):
    (qat_ref, qbt_ref, ka_ref, kb_ref, kat_ref, vat_ref, kbt_ref, vbt_ref,
     vat16_ref, vbt16_ref, gb_ref, u_ref, lft_ref) = refs[n_carried:]
    h = _rms(x_ref[...], g_ref[...]).astype(BF16)
    tm = h.shape[0]
    rows = lambda k: wt_ref[WIDTH * k:WIDTH * (k + 1), :]
    for q_ref, kt_ref, k_ref, vt_ref, vt16_ref, base in ((qat_ref, kat_ref, ka_ref, vat_ref, vat16_ref, 0),
                                                         (qbt_ref, kbt_ref, kb_ref, vbt_ref, vbt16_ref, 3)):
        qt = (_nt(rows(base), h) * (SCALE * LOG2E)).astype(BF16)
        kt_ref[...] = _nt(rows(base + 1), h)
        k_ref[...] = _nt(h, rows(base + 1)).astype(BF16)
        vt = _nt(rows(base + 2), h)
        vt_ref[...] = vt
        for c in range(tm // tq):
            q_ref[c] = qt[:, tq * c:tq * (c + 1)]
            vt16_ref[c] = vt[:, tq * c:tq * (c + 1)].astype(BF16)
    gb_ref[...] = _nn(h, wc_ref[:, 0:WIDTH]).astype(BF16)
    u_ref[...] = _nn(h, wc_ref[:, WIDTH:2 * WIDTH]) * _nn(h, wc_ref[:, 2 * WIDTH:3 * WIDTH])
    ft = _nt(wft_ref[...], h)[:N_HEADS, :]
    lft_ref[...] = _log_sigmoid(ft + bft_ref[...])


N_NEW_KV = 4
FIRST_NEW_KV_OUT = 4


def _in_proj_prompt(layer, depth, carried, x, g, wt, wc, wft, b_f, *, n_seq, t_len, tm, tq):
    m = x.shape[0]
    tps = t_len // tm
    nq = t_len // tq
    row = lambda i: (i, 0)
    const = lambda i: (0, 0)
    tok = lambda dt: (jax.ShapeDtypeStruct((m, WIDTH), dt), pl.BlockSpec((tm, WIDTH), row))
    new_t = (jax.ShapeDtypeStruct((depth, n_seq, WIDTH, t_len), F32),
             pl.BlockSpec((None, None, WIDTH, tm), lambda i: (layer, i // tps, 0, i % tps)))
    blocked = (jax.ShapeDtypeStruct((n_seq, nq, WIDTH, tq), BF16),
               pl.BlockSpec((None, tm // tq, WIDTH, tq), lambda i: (i // tps, i % tps, 0, 0)))
    outs = [blocked, blocked, tok(BF16), tok(BF16)] + [new_t] * N_NEW_KV + [blocked, blocked, tok(BF16), tok(F32),
            (jax.ShapeDtypeStruct((n_seq, N_HEADS, t_len), F32),
             pl.BlockSpec((None, N_HEADS, tm), lambda i: (i // tps, 0, i % tps)))]
    in_specs = [pl.BlockSpec((tm, D_MODEL), row),
                pl.BlockSpec((1, D_MODEL), const),
                pl.BlockSpec(wt.shape, const),
                pl.BlockSpec(wc.shape, const),
                pl.BlockSpec(wft.shape, const),
                pl.BlockSpec((N_HEADS, 1), const)]
    assert len(carried) in (0, N_NEW_KV)
    aliases = {len(in_specs) + k: FIRST_NEW_KV_OUT + k for k in range(len(carried))}
    return pl.pallas_call(
        functools.partial(_in_proj_prompt_kernel, tq=tq, n_carried=len(carried)),
        grid=(m // tm,),
        in_specs=in_specs + [pl.BlockSpec(memory_space=pl.ANY)] * len(carried),
        out_specs=[o[1] for o in outs],
        out_shape=[o[0] for o in outs],
        input_output_aliases=aliases,
        compiler_params=_params(1),
        name="in_proj_prompt",
    )(x, g, wt, wc, wft, b_f.reshape(N_HEADS, 1), *carried)


def _in_proj_sample_kernel(x_ref, g_ref, wt_ref, wc_ref, wf_ref, wft_ref, bf_ref, bft_ref,
                           qa_ref, ka_ref, va_ref, qb_ref, kb_ref, vb_ref, gb_ref, u_ref, lf_ref, lft_ref):
    h = _rms(x_ref[...], g_ref[...]).astype(BF16)
    for k, o_ref in enumerate((qa_ref, ka_ref, va_ref, qb_ref, kb_ref, vb_ref)):
        y = _nt(h, wt_ref[WIDTH * k:WIDTH * (k + 1), :])
        o_ref[...] = y * SCALE if k % 3 == 0 else y
    gb_ref[...] = _nn(h, wc_ref[:, 0:WIDTH]).astype(BF16)
    u_ref[...] = _nn(h, wc_ref[:, WIDTH:2 * WIDTH]) * _nn(h, wc_ref[:, 2 * WIDTH:3 * WIDTH])
    f = _nn(h, wf_ref[...])[:, :N_HEADS]
    lf_ref[...] = _log_sigmoid(f + bf_ref[...])
    ft = _nt(wft_ref[...], h)[:N_HEADS, :]
    lft_ref[...] = _log_sigmoid(ft + bft_ref[...])


def _in_proj_sample(x, g, wt, wc, wf, wft, b_f):
    m = x.shape[0]
    full = lambda shape: pl.BlockSpec(shape, lambda i: (0,) * len(shape))
    tok = lambda dt: jax.ShapeDtypeStruct((m, WIDTH), dt)
    out_shape = [tok(F32)] * 6 + [tok(BF16), tok(F32),
                                  jax.ShapeDtypeStruct((m, N_HEADS), F32), jax.ShapeDtypeStruct((N_HEADS, m), F32)]
    args = (x, g, wt, wc, wf, wft, b_f.reshape(1, N_HEADS), b_f.reshape(N_HEADS, 1))
    return pl.pallas_call(
        _in_proj_sample_kernel,
        grid=(1,),
        in_specs=[full(a.shape) for a in args],
        out_specs=[full(s.shape) for s in out_shape],
        out_shape=out_shape,
        compiler_params=_params(1),
        name="in_proj_sample",
    )(*args)


def _cumsum_kernel(lt_ref, c_ref, ct_ref, *, t_len, tq):
    nb = t_len // LANES
    x = jnp.concatenate([lt_ref[:, LANES * k:LANES * (k + 1)] for k in range(nb)], axis=0)
    n = nb * N_HEADS
    r = lax.broadcasted_iota(jnp.int32, (LANES, LANES), 0)
    c = lax.broadcasted_iota(jnp.int32, (LANES, LANES), 1)
    incl = (r <= c).astype(BF16)
    y = sum(_nn(p, incl) for p in _split_bf16(x, 3))
    r = lax.broadcasted_iota(jnp.int32, (n, n), 0)
    c = lax.broadcasted_iota(jnp.int32, (n, n), 1)
    earlier = ((c // N_HEADS < r // N_HEADS) & (c % N_HEADS == r % N_HEADS)).astype(BF16)
    offs = sum(_nn(earlier, p) for p in _split_bf16(y, 3))[:, LANES - 1:LANES]
    z = (y + offs) * LOG2E
    zt = z.T
    per_q = tq // LANES
    for k in range(nb):
        ct_ref[k // per_q, :, LANES * (k % per_q):LANES * (k % per_q + 1)] = z[N_HEADS * k:N_HEADS * (k + 1), :]
        c_ref[LANES * k:LANES * (k + 1), :] = zt[:, N_HEADS * k:N_HEADS * (k + 1)]


def _cumsum(logf_t, *, tq):
    n_seq, _, t_len = logf_t.shape
    nq = t_len // tq
    return pl.pallas_call(
        functools.partial(_cumsum_kernel, t_len=t_len, tq=tq),
        grid=(n_seq,),
        in_specs=[pl.BlockSpec((None, N_HEADS, t_len), lambda b: (b, 0, 0))],
        out_specs=[pl.BlockSpec((t_len, N_HEADS), lambda b: (b, 0)),
                   pl.BlockSpec((None, nq, N_HEADS, tq), lambda b: (b, 0, 0, 0))],
        out_shape=[jax.ShapeDtypeStruct((n_seq * t_len, N_HEADS), F32),
                   jax.ShapeDtypeStruct((n_seq, nq, N_HEADS, tq), F32)],
        compiler_params=_params(1),
        name="logf_cumsum",
    )(logf_t)


def _head_pair_qt(qt):
    low =lax.broadcasted_iota(jnp.int32, qt.shape, 0) < HEAD_DIM
    zero = jnp.zeros_like(qt)
    return jnp.where(low, qt, zero), jnp.where(low, zero, qt)


def _head_column(c, head):
    hcol = lax.broadcasted_iota(jnp.int32, c.shape, 1)
    return jnp.sum(jnp.where(hcol == head, c, 0.0), axis=1, keepdims=True)


def _head_row(ct, head):
    hrow = lax.broadcasted_iota(jnp.int32, ct.shape, 0)
    return jnp.sum(jnp.where(hrow == head, ct, 0.0), axis=0, keepdims=True)


def _fox_kernel(qt_ref, k_ref, vt_ref, c_ref, ct_ref, o_ref, ck_s, s_s, *, tq):
    pair = pl.program_id(1)
    c_all = c_ref[...]
    for e in (0, 1):
        ck_s[e] = jnp.broadcast_to(_head_column(c_all, 2 * pair + e), ck_s.shape[1:])

    def query_block(qi, carry):
        o_ref[pl.ds(pl.multiple_of(qi * tq, tq), tq), :] = _fox_query_block(
            qi, pair, qt_ref[qi], ct_ref[qi], k_ref, vt_ref, ck_s, s_s, tq).astype(o_ref.dtype)
        return carry

    lax.fori_loop(0, qt_ref.shape[0], query_block, 0)


def _fox_query_block(qi, pair, qt, cq_all, k_ref, vt_ref, ck_s, s_s, tq):
    qts = _head_pair_qt(qt)
    cq = [_head_row(cq_all, 2 * pair + e) for e in (0, 1)]
    key = lax.broadcasted_iota(jnp.int32, (tq, tq), 0)
    qry = lax.broadcasted_iota(jnp.int32, (tq, tq), 1)
    reps = tq // LANES
    n_full = qi // 2

    def logits(js, m, masked):
        starts = [pl.multiple_of(j * tq, tq) for j in js]
        qk = [[_nn(k_ref[pl.ds(st, tq), :], qts[e]) for e in (0, 1)] for st in starts]
        m = list(m)
        for b, (j, st) in enumerate(zip(js, starts)):
            for e in (0, 1):
                ck = ck_s[e, pl.ds(st, tq), :]
                s = qk[b][e] + (cq[e] - jnp.concatenate([ck] * reps, axis=1))
                if masked:
                    s = jnp.where(key + (j - qi) * tq <= qry, s, NEG)
                s_s[e, j] = s
                m[e] = jnp.maximum(m[e], jnp.max(s, axis=0, keepdims=True))
        return tuple(m)

    even = qi % 2 == 0
    m = lax.fori_loop(0, n_full, lambda i, m: logits((2 * i, 2 * i + 1), m, False),
                      (jnp.full((1, tq), NEG, F32),) * 2)
    m = lax.cond(even, lambda m: logits((qi,), m, True), lambda m: logits((qi - 1, qi), m, True), m)

    ones = jnp.ones((2 * SUBLANES, tq), BF16)

    def weigh(js, acc):
        p = [[jnp.exp2(s_s[e, j] - m[e]).astype(BF16) for e in (0, 1)] for j in js]
        pv = [[_nn(jnp.concatenate([vt_ref[j, HEAD_DIM * e:HEAD_DIM * (e + 1), :], ones], axis=0), p[b][e])
               for e in (0, 1)] for b, j in enumerate(js)]
        return tuple(acc[e] + sum(x[e] for x in pv) for e in (0, 1))

    acc = lax.fori_loop(0, n_full, lambda i, acc: weigh((2 * i, 2 * i + 1), acc),
                        (jnp.zeros((HEAD_DIM + 2 * SUBLANES, tq), F32),) * 2)
    acc = lax.cond(even, lambda acc: weigh((qi,), acc), lambda acc: weigh((qi - 1, qi), acc), acc)
    out = [acc[e][0:HEAD_DIM] / acc[e][HEAD_DIM:HEAD_DIM + 1] for e in (0, 1)]
    return jnp.concatenate(out, axis=0).T


def _sb_kernel(qt_ref, k_ref, vt_ref, tri_ref, o_ref, lr_s, pos_s, *, tq):
    def query_block(qi, carry):
        o_ref[pl.ds(pl.multiple_of(qi * tq, tq), tq), :] = _sb_query_block(
            qi, qt_ref[qi], k_ref, vt_ref, tri_ref, lr_s, pos_s, tq).astype(o_ref.dtype)
        return carry

    lax.fori_loop(0, qt_ref.shape[0], query_block, 0)


def _sb_query_block(qi, qt, k_ref, vt_ref, tri_ref, lr_s, pos_s, tq):
    nqts = [-x for x in _head_pair_qt(qt)]
    key = lax.broadcasted_iota(jnp.int32, (tq, tq), 0)
    qry = lax.broadcasted_iota(jnp.int32, (tq, tq), 1)
    n_full = qi // 2

    def gates(js, masked):
        nz = [[_nn(k_ref[pl.ds(pl.multiple_of(j * tq, tq), tq), :], nqts[e]) for e in (0, 1)] for j in js]
        for b, j in enumerate(js):
            for e in (0, 1):
                n = nz[b][e]
                neg_abs = lax.bitcast_convert_type(lax.bitcast_convert_type(n, jnp.int32) | SIGN_BIT, F32)
                lr = jnp.minimum(n, 0.0) - jnp.log2(1.0 + jnp.exp2(neg_abs))
                ls_pos = lr - n
                if masked:
                    strict = key + (j - qi) * tq < qry
                    lr = jnp.where(strict, lr, 0.0)
                    ls_pos = jnp.where(strict, ls_pos, NEG)
                lr_s[e, j] = lr.astype(BF16)
                pos_s[e, j] = ls_pos

    def unmasked(i, carry):
        gates((2 * i, 2 * i + 1), False)
        return carry

    even = qi % 2 == 0
    lax.fori_loop(0, n_full, unmasked, 0)

    @pl.when(even)
    def _():
        gates((qi,), True)

    @pl.when(jnp.logical_not(even))
    def _():
        gates((qi - 1, qi), True)

    tri = tri_ref[...]

    def weigh(js, carry):
        sums = [[_nn(tri, lr_s[e, j]) for e in (0, 1)] for j in js]
        rest =[[carry[e][0]] for e in (0, 1)]
        for b in range(len(js)):
            for e in (0, 1):
                rest[e].append(rest[e][b] + sums[b][e][tq:tq + 1])
        a = [[jnp.exp2(pos_s[e, j] + (sums[b][e][0:tq] + rest[e][b])).astype(BF16) for e in (0, 1)]
             for b, j in enumerate(js)]
        pv = [[_nn(vt_ref[j, HEAD_DIM * e:HEAD_DIM * (e + 1), :], a[b][e]) for e in (0, 1)]
              for b, j in enumerate(js)]
        return tuple((rest[e][-1], carry[e][1] + sum(x[e] for x in pv)) for e in (0, 1))

    init = ((jnp.zeros((1, tq), F32), jnp.zeros((HEAD_DIM, tq), F32)),) * 2
    carry = lax.cond(even, lambda c: weigh((qi,), c), lambda c: weigh((qi, qi - 1), c), init)
    top = 2 * n_full - 1
    (_, a0), (_, a1) = lax.fori_loop(0, n_full, lambda i, c: weigh((top - 2 * i, top - 2 * i - 1), c), carry)
    return jnp.concatenate([a0, a1], axis=0).T


def _prompt_attention(kernel_fn, name, qt16, k, vt16, extra, extra_specs, scratch, *, tq):
    n_seq, nq = vt16.shape[:2]
    t_len = nq * tq
    n_pair = WIDTH // LANES
    blocked = pl.BlockSpec((None, nq, LANES, tq), lambda b, p: (b, 0, p, 0))
    tokens = pl.BlockSpec((t_len, LANES), lambda b, p: (b, p))
    return pl.pallas_call(
        functools.partial(kernel_fn, tq=tq),
        grid=(n_seq, n_pair),
        in_specs=[blocked, tokens, blocked] + extra_specs,
        out_specs=tokens,
        out_shape=jax.ShapeDtypeStruct(k.shape, BF16),
        scratch_shapes=scratch,
        compiler_params=_params(2),
        name=name,
    )(qt16, k, vt16, *extra)


def _sample_attn_kernel(pt_ref, qa_ref, qb_ref, kan_ref, van_ref, kbn_ref, vbn_ref, lfn_ref,
                        tri_ref, hm_ref, *rest, n_pg, dec_b, dec_t):
    del pt_ref
    pg_ak, pg_av, pg_bk, pg_bv, pg_lf = (rest[n_pg * i:n_pg * (i + 1)] for i in range(5))
    oa_ref, ob_ref = rest[5 * n_pg:5 * n_pg + 2]
    qa_s, qb_s, acca_s, accb_s, m_s, l_s, dec_s, rest_s = rest[5 * n_pg + 2:]
    b = pl.program_id(0)
    j = pl.program_id(1)
    nr = dec_t * N_HEADS
    hm = hm_ref[...]

    def token_rows(ref):
        return [ref[pl.ds(t * dec_b + b, 1), :] for t in range(dec_t)]

    def bf16_exact(x):
        return x.astype(BF16).astype(F32)

    def lanes(xs):
        return xs[0] if len(xs) == 1 else jnp.concatenate(xs, axis=1)

    def update(blocks, own):
        n = PAGE_SIZE * len(blocks[0][0])
        rowsum = lambda x: jnp.sum(x, axis=1, keepdims=True)
        sa = [_nn(qa_s[...], lanes(blk[0])) for blk in blocks]
        sb = [_nn(qb_s[...], lanes(blk[2])) for blk in blocks]
        lf = [jnp.concatenate([lanes(blk[4])] * dec_t, axis=0) for blk in blocks]
        ls = [_log_sigmoid_pair(x) for x in sb]
        if own:
            key = lax.broadcasted_iota(jnp.int32, (nr, n), 1)
            t_row = lax.broadcasted_iota(jnp.int32, (nr, n), 0) // N_HEADS
            valid_a = key <= t_row
            valid_b = key < t_row
            lr = [jnp.where(valid_b, x[0], 0.0) for x in ls]
        else:
            lr = [x[0] for x in ls]
        tri = tri_ref[0:n, 0:n]
        later = [_nn(jnp.concatenate(_split_bf16(f, 2) + _split_bf16(r, 2), axis=0), tri)
                 for f, r in zip(lf, lr)]
        later_a = [x[0:nr] + x[nr:2 * nr] for x in later]
        later_b = [x[2 * nr:3 * nr] + x[3 * nr:4 * nr] for x in later]

        dec = [-rowsum(jnp.where(key == t_row, later_a[0], 0.0)) if own else dec_s[...]]
        rest = [rest_s[...]]
        for f, r in zip(lf, lr):
            dec.append(dec[-1] + rowsum(f))
            rest.append(rest[-1] + rowsum(r))
        s = [x + (la + d) for x, la, d in zip(sa, later_a, dec)]
        if own:
            s = [jnp.where(valid_a, x, NEG) for x in s]
        m_old = m_s[...]
        m_new = functools.reduce(jnp.maximum, [m_old] + [jnp.max(x, axis=1, keepdims=True) for x in s])
        alpha = jnp.exp(m_old - m_new)
        p = [jnp.exp(x - m_new) for x in s]
        a = [jnp.exp(x[1] + lb + r) for x, lb, r in zip(ls, later_b, rest)]
        if own:
            a = [jnp.where(valid_b, x, 0.0) for x in a]
        pv_a = sum(_nt(bf16_exact(x), lanes(blk[1])) for x, blk in zip(p, blocks))
        pv_b = sum(_nt(bf16_exact(x), lanes(blk[3])) for x, blk in zip(a, blocks))
        l_s[...] = alpha * l_s[...] + sum(rowsum(x) for x in p)
        acca_s[...] = alpha * acca_s[...] + pv_a
        m_s[...] = m_new
        dec_s[...] = dec[-1]
        accb_s[...] = accb_s[...] + pv_b
        rest_s[...] = rest[-1]

    @pl.when(j == 0)
    def _():
        def block_diag_q(ref):
            return bf16_exact(jnp.concatenate(
                [jnp.broadcast_to(r, (N_HEADS, WIDTH)) * hm for r in token_rows(ref)], axis=0))

        def own_tile(ref):
            sub = lax.broadcasted_iota(jnp.int32, (SUBLANES, WIDTH), 0)
            top = jnp.zeros((SUBLANES, WIDTH), F32)
            for t, r in enumerate(token_rows(ref)):
                top = jnp.where(sub == t, jnp.broadcast_to(r, (SUBLANES, WIDTH)), top)
            return jnp.concatenate([top, jnp.zeros((PAGE_SIZE - SUBLANES, WIDTH), F32)], axis=0).T

        qa_s[...] = block_diag_q(qa_ref)
        qb_s[...] = block_diag_q(qb_ref)
        lane = lax.broadcasted_iota(jnp.int32, (N_HEADS, dec_b * dec_t), 1)
        pos = lax.broadcasted_iota(jnp.int32, (N_HEADS, PAGE_SIZE), 1)
        lfn = lfn_ref[...]
        lf0 = jnp.zeros((N_HEADS, PAGE_SIZE), F32)
        for t in range(dec_t):
            col = jnp.sum(jnp.where(lane == t * dec_b + b, lfn, 0.0), axis=1, keepdims=True)
            lf0 = jnp.where(pos == t, col, lf0)
        acca_s[...] = jnp.zeros((nr, WIDTH), F32)
        accb_s[...] = jnp.zeros((nr, WIDTH), F32)
        m_s[...] = jnp.full((nr, 1), NEG, F32)
        l_s[...] = jnp.zeros((nr, 1), F32)
        dec_s[...] = jnp.zeros((nr, 1), F32)
        rest_s[...] = jnp.zeros((nr, 1), F32)
        update([([own_tile(kan_ref)], [own_tile(van_ref)], [own_tile(kbn_ref)], [own_tile(vbn_ref)], [lf0])], True)

    @pl.when(j > 0)
    def _():
        update([[[pg[...] for pg in pgs[hi - PAGES_PER_BLOCK:hi]] for pgs in (pg_ak, pg_av, pg_bk, pg_bv, pg_lf)]
                for hi in range(n_pg, 0, -PAGES_PER_BLOCK)], False)

    @pl.when(j == pl.num_programs(1) - 1)
    def _():
        oa = acca_s[...] / l_s[...]
        ob = accb_s[...]
        for t in range(dec_t):
            rows = slice(N_HEADS * t, N_HEADS * (t + 1))
            oa_ref[pl.ds(t * dec_b + b, 1), :] = jnp.sum(oa[rows] * hm, axis=0, keepdims=True)
            ob_ref[pl.ds(t * dec_b + b, 1), :] = jnp.sum(ob[rows] * hm, axis=0, keepdims=True)


def _sample_attention(layer, page_table, qa, qb, kan, van, kbn, vbn, lfn_t,
                      cache_a_kt, cache_a_vt, cache_a_lft, cache_b_kt, cache_b_vt, *, dec_b, dec_t):
    n_pg = PAGES_PER_STEP
    n_pages = page_table.shape[0] // dec_b
    nk = PAGES_PER_BLOCK * PAGE_SIZE
    nr = dec_t * N_HEADS
    m = dec_b * dec_t
    r = lax.broadcasted_iota(jnp.int32, (nk, nk), 0)
    c = lax.broadcasted_iota(jnp.int32, (nk, nk), 1)
    tri = (r > c).astype(BF16)
    hm = (lax.broadcasted_iota(jnp.int32, (N_HEADS, WIDTH), 1) // HEAD_DIM
          == lax.broadcasted_iota(jnp.int32, (N_HEADS, WIDTH), 0)).astype(F32)

    def page_spec(i, rows):
        def index(b, j, pt):
            pg = n_pages - jnp.maximum(j, 1) * n_pg + i
            return (layer, pt[b * n_pages + pg], 0, 0)
        return pl.BlockSpec((None, None, rows, PAGE_SIZE), index)

    full = lambda shape: pl.BlockSpec(shape, lambda b, j, pt: (0,) * len(shape))
    caches = (cache_a_kt, cache_a_vt, cache_b_kt, cache_b_vt, cache_a_lft)
    heights = (WIDTH, WIDTH, WIDTH, WIDTH, N_HEADS)
    page_specs = [page_spec(i, h) for h in heights for i in range(n_pg)]
    page_args = [cch for cch in caches for _ in range(n_pg)]
    grid_spec = pltpu.PrefetchScalarGridSpec(
        num_scalar_prefetch=1,
        grid=(dec_b, 1 + n_pages // n_pg),
        in_specs=[full((m, WIDTH))] * 6 + [full((N_HEADS, m)), full((nk, nk)), full((N_HEADS, WIDTH))]
        + page_specs,
        out_specs=[full((m, WIDTH)), full((m, WIDTH))],
        scratch_shapes=[pltpu.VMEM((nr, WIDTH), F32)] * 4 + [pltpu.VMEM((nr, 1), F32)] * 4,
    )
    return pl.pallas_call(
        functools.partial(_sample_attn_kernel, n_pg=n_pg, dec_b=dec_b, dec_t=dec_t),
        grid_spec=grid_spec,
        out_shape=[jax.ShapeDtypeStruct((m, WIDTH), F32)] * 2,
        compiler_params=_params(2),
        name="sample_attention",
    )(page_table, qa, qb, kan, van, kbn, vbn, lfn_t, tri, hm, *page_args)


def _merge_kernel(x_ref, oa_ref, ob_ref, gb_ref, u_ref, uprev_ref, st_ref, cw_ref, gpre_ref, gpost_ref,
                  wg_ref, bg_ref, wpa_ref, wpb_ref, wpc_ref, wo_ref, xo_ref, nb_ref, ext_s,
                  *, tm, shift, halo, tiles_per_seq):
    i = pl.program_id(0)
    x = x_ref[...]
    h = _rms(x, gpre_ref[...]).astype(BF16)
    u = u_ref[...]
    seq_start = i % tiles_per_seq == 0
    ext_s[0:halo, :] = jnp.where(seq_start, st_ref[0], uprev_ref[...])
    ext_s[halo:halo + tm, :] = u
    conv = (cw_ref[0:1, :] * ext_s[halo - 2 * shift:halo - 2 * shift + tm, :]
            + cw_ref[1:2, :] * ext_s[halo - shift:halo - shift + tm, :]
            + cw_ref[2:3, :] * u)
    yc = gb_ref[...].astype(F32) * conv
    mixed = jnp.zeros((tm, D_MODEL), F32)
    branches = ((oa_ref[...].astype(BF16), wpa_ref), (ob_ref[...].astype(BF16), wpb_ref), (yc.astype(BF16), wpc_ref))
    for br, (o, wp_ref) in enumerate(branches):
        cols = slice(D_MODEL * br, D_MODEL * (br + 1))
        gate = jax.nn.sigmoid(_nn(h, wg_ref[:, cols]) + bg_ref[:, cols])
        mixed = mixed + gate * _nn(o, wp_ref[...])
    y = _nn(mixed.astype(BF16), wo_ref[...])
    xo_ref[...] = x + _rms(y, gpost_ref[...])

    @pl.when(i % tiles_per_seq == tiles_per_seq - 1)
    def _():
        nb_ref[0] = ext_s[halo + tm - 2 * shift:halo + tm, :]


def _merge(x, oa, ob, gb, u, state_halo, conv_w, g_pre, g_post, w_gate, b_gate, wpa, wpb, wpc, wo,
           *, tm, shift, tiles_per_seq):
    m = x.shape[0]
    halo = state_halo.shape[1]
    n_groups = m // (tm * tiles_per_seq)
    row = lambda i: (i, 0)
    const = lambda i: (0, 0)
    wide = pl.BlockSpec((tm, WIDTH), row)
    return pl.pallas_call(
        functools.partial(_merge_kernel, tm=tm, shift=shift, halo=halo, tiles_per_seq=tiles_per_seq),
        grid=(m // tm,),
        in_specs=[pl.BlockSpec((tm, D_MODEL), row), wide, wide, wide, wide,
                  pl.BlockSpec((halo, WIDTH), lambda i: (jnp.maximum(i * (tm // halo) - 1, 0), 0)),
                  pl.BlockSpec((1, halo, WIDTH), lambda i: (i // tiles_per_seq, 0, 0)),
                  pl.BlockSpec((CONV_W, WIDTH), const),
                  pl.BlockSpec((1, D_MODEL), const), pl.BlockSpec((1, D_MODEL), const),
                  pl.BlockSpec(w_gate.shape, const), pl.BlockSpec((1, N_BRANCH * D_MODEL), const),
                  pl.BlockSpec(wpa.shape, const), pl.BlockSpec(wpb.shape, const),
                  pl.BlockSpec(wpc.shape, const), pl.BlockSpec(wo.shape, const)],
        out_specs=[pl.BlockSpec((tm, D_MODEL), row),
                   pl.BlockSpec((1, 2 * shift, WIDTH), lambda i: (i // tiles_per_seq, 0, 0))],
        out_shape=[jax.ShapeDtypeStruct((m, D_MODEL), F32),
                   jax.ShapeDtypeStruct((n_groups, 2 * shift, WIDTH), F32)],
        scratch_shapes=[pltpu.VMEM((halo + tm, WIDTH), F32)],
        compiler_params=_params(1),
        name="merge",
    )(x, oa, ob, gb, u, u, state_halo, conv_w, g_pre, g_post, w_gate, b_gate, wpa, wpb, wpc, wo)


def _ffn_kernel(x_ref, gpre_ref, gpost_ref, wg_ref, wu_ref, wd_ref, o_ref, h_s, acc_s):
    j = pl.program_id(1)

    @pl.when(j == 0)
    def _():
        h_s[...] = _rms(x_ref[...], gpre_ref[...]).astype(BF16)
        acc_s[...] = jnp.zeros_like(acc_s)

    h = h_s[...]
    g = _nn(h, wg_ref[...])
    up = _nn(h, wu_ref[...])
    act = (g * jax.nn.sigmoid(g) * up).astype(BF16)
    acc_s[...] += _nn(act, wd_ref[...])

    @pl.when(j == pl.num_programs(1) - 1)
    def _():
        o_ref[...] = x_ref[...] + _rms(acc_s[...], gpost_ref[...])


def _ffn(x, g_pre, g_post, wg, wu, wd, *, tm, tf):
    m = x.shape[0]
    row = lambda i, j: (i, 0)
    const = lambda i, j: (0, 0)
    return pl.pallas_call(
        _ffn_kernel,
        grid=(m // tm, D_FF // tf),
        in_specs=[pl.BlockSpec((tm, D_MODEL), row),
                  pl.BlockSpec((1, D_MODEL), const), pl.BlockSpec((1, D_MODEL), const),
                  pl.BlockSpec((D_MODEL, tf), lambda i, j: (0, j)),
                  pl.BlockSpec((D_MODEL, tf), lambda i, j: (0, j)),
                  pl.BlockSpec((tf, D_MODEL), lambda i, j: (j, 0))],
        out_specs=pl.BlockSpec((tm, D_MODEL), row),
        out_shape=jax.ShapeDtypeStruct((m, D_MODEL), F32),
        scratch_shapes=[pltpu.VMEM((tm, D_MODEL), BF16), pltpu.VMEM((tm, D_MODEL), F32)],
        compiler_params=_params(2),
        name="ffn",
    )(x, g_pre, g_post, wg, wu, wd)


def kernel(x_prompt, x_sample, cache_a_k, cache_a_v, cache_a_logf, cache_b_k, cache_b_v, state_conv, page_table, w_in, b_f, conv_w, w_proj_a, w_proj_b, w_proj_c, w_gate, b_gate, w_o, g_mix_pre, g_mix_post, g_ffn_pre, g_ffn_post, w_ffn_gate, w_ffn_up, w_ffn_down):
    n_seq, t_len, _ = x_prompt.shape
    dec_b, dec_t, _ = x_sample.shape
    depth = w_in.shape[0]
    n_pool = cache_a_k.shape[1]
    mp = n_seq * t_len
    ms = dec_b * dec_t
    tq = ATT_BLOCK
    tm_p = 512

    xp = x_prompt.reshape(mp, D_MODEL)
    xs = jnp.transpose(x_sample, (1, 0, 2)).reshape(ms, D_MODEL)
    pt = page_table.reshape(-1)
    feature_major = lambda c: jnp.transpose(c, (0, 1, 3, 4, 2)).reshape(depth, n_pool, WIDTH, PAGE_SIZE)
    caches = (feature_major(cache_a_k), feature_major(cache_a_v), jnp.transpose(cache_a_logf, (0, 1, 3, 2)),
              feature_major(cache_b_k), feature_major(cache_b_v))
    r = lax.broadcasted_iota(jnp.int32, (tq + 2 * SUBLANES, tq), 0)
    c = lax.broadcasted_iota(jnp.int32, (tq + 2 * SUBLANES, tq), 1)
    tri = ((c > r) | (r >= tq)).astype(BF16)
    prompt_halo = jnp.zeros((n_seq, SUBLANES, WIDTH), F32)

    rows_p, rows_s = [], []
    new_kv = [jnp.zeros((depth, n_seq, WIDTH, t_len), F32) for _ in range(N_NEW_KV)]
    qkv = 3 * WIDTH
    for l in range(depth):
        wl = w_in[l]
        wt = jnp.concatenate([wl[:, 0:qkv], wl[:, qkv + N_HEADS:2 * qkv + N_HEADS]], axis=1).T.astype(BF16)
        wc = wl[:, 2 * qkv + N_HEADS:].astype(BF16)
        w_f = wl[:, qkv:qkv + N_HEADS]
        wf = jnp.pad(w_f, ((0, 0), (0, LANES - N_HEADS))).astype(BF16)
        wft = jnp.pad(w_f.T, ((0, 2 * SUBLANES - N_HEADS), (0, 0))).astype(BF16)
        g_pre, g_post = g_mix_pre[l].reshape(1, -1), g_mix_post[l].reshape(1, -1)
        f_pre, f_post = g_ffn_pre[l].reshape(1, -1), g_ffn_post[l].reshape(1, -1)
        merge_w = (conv_w[l], g_pre, g_post, w_gate[l].astype(BF16), b_gate[l].reshape(1, -1),
                   w_proj_a[l].astype(BF16), w_proj_b[l].astype(BF16), w_proj_c[l].astype(BF16),
                   w_o[l].astype(BF16))
        ffn_w = (f_pre, f_post, w_ffn_gate[l].astype(BF16), w_ffn_up[l].astype(BF16),
                 w_ffn_down[l].astype(BF16))

        (qat, qbt, ka16, kb16, *new_kv, vat16, vbt16, gb, u, logf_t) = _in_proj_prompt(
            l, depth, new_kv, xp, g_pre, wt, wc, wft, b_f[l], n_seq=n_seq, t_len=t_len, tm=tm_p, tq=tq)
        cum, cum_t = _cumsum(logf_t, tq=tq)
        nq = t_len // tq
        oa = _prompt_attention(
            _fox_kernel, "fox_attention", qat, ka16, vat16, (cum, cum_t),
            [pl.BlockSpec((t_len, N_HEADS), lambda b, p: (b, 0)),
             pl.BlockSpec((None, nq, N_HEADS, tq), lambda b, p: (b, 0, 0, 0))],
            [pltpu.VMEM((2, t_len, LANES), F32), pltpu.VMEM((2, nq, tq, tq), F32)], tq=tq)
        ob = _prompt_attention(
            _sb_kernel, "sb_attention", qbt, kb16, vbt16, (tri,),
            [pl.BlockSpec(tri.shape, lambda b, p: (0, 0))],
            [pltpu.VMEM((2, nq, tq, tq), BF16), pltpu.VMEM((2, nq, tq, tq), F32)], tq=tq)
        xp, nb_p = _merge(xp, oa, ob, gb, u, prompt_halo, *merge_w,
                          tm=tm_p, shift=1, tiles_per_seq=t_len // tm_p)
        xp = _ffn(xp, *ffn_w, tm=1024, tf=256)
        rows_p.append((logf_t, nb_p))

        (qa, ka, va, qb, kb, vb, gb, u, logf, logf_t) = _in_proj_sample(
            xs, g_pre, wt, wc, wf, wft, b_f[l])
        oa, ob = _sample_attention(l, pt, qa, qb, ka, va, kb, vb, logf_t, *caches,
                                   dec_b=dec_b, dec_t=dec_t)
        state = jnp.transpose(state_conv[l], (1, 0, 2)).reshape(1, (CONV_W - 1) * dec_b, WIDTH)
        xs, nb_s = _merge(xs, oa, ob, gb, u, state, *merge_w, tm=ms, shift=dec_b, tiles_per_seq=1)
        xs = _ffn(xs, *ffn_w, tm=ms, tf=256)
        rows_s.append((ka, va, logf, kb, vb, nb_s))

    stack = lambda rows, i: jnp.stack([r[i] for r in rows], axis=0)

    def new_kv_p(i):
        a = new_kv[i].reshape(depth, n_seq, N_HEADS, HEAD_DIM, t_len)
        return jnp.transpose(a, (0, 1, 4, 2, 3))

    def stack_s(i, tail):
        a = stack(rows_s, i).reshape(depth, -1, dec_b, *tail)
        return jnp.swapaxes(a, 1, 2)

    heads = (N_HEADS, HEAD_DIM)
    return (xp.reshape(n_seq, t_len, D_MODEL),
            jnp.transpose(xs.reshape(dec_t, dec_b, D_MODEL), (1, 0, 2)),
            new_kv_p(0), new_kv_p(1), jnp.transpose(stack(rows_p, 0), (0, 1, 3, 2)),
            new_kv_p(2), new_kv_p(3), stack(rows_p, 1),
            stack_s(0, heads), stack_s(1, heads), stack_s(2, (N_HEADS,)),
            stack_s(3, heads), stack_s(4, heads), stack_s(5, (WIDTH,)))
```
